```python
import math
import jax, jax.numpy as jnp
from jax import lax
import numpy as np

D_MODEL = 1024
BATCH = 32
SEQ = 2048
DEPTH = 1

N_HEADS = 8
HEAD_DIM = 64
V_DIM = 2 * HEAD_DIM
ATTN_QK = N_HEADS * 2 * HEAD_DIM
ATTN_V = N_HEADS * V_DIM
Q_BLOCK = 128
POOL_WINDOWS = (2, 4, 8, 16)
N_POOL_GROUPS = len(POOL_WINDOWS)
POOL_GROUP_IN = 128
POOL_IN = N_POOL_GROUPS * POOL_GROUP_IN
POOL_GROUP_OUT = D_MODEL // N_POOL_GROUPS
N_BRANCHES = 2
IN_COLS = 2 * ATTN_QK + ATTN_V + POOL_IN + N_BRANCHES * D_MODEL
D_FF = -(-8 * D_MODEL // (3 * 256)) * 256
ALPHA = (2 * DEPTH) ** 0.25
BETA = (8 * DEPTH) ** -0.25
LN_EPS = 1e-5
RMS_EPS = 1e-5
N_MOD = 6

kernel_name = "hybrid_diffattn_multipool_deepnorm_block"


def alibi_slopes(n_heads):
    return np.array([2.0 ** (-8.0 * (h + 1) / n_heads) for h in range(n_heads)], dtype=np.float32)


def layer_norm(x, g=None, b=None):
    xf = x.astype(jnp.float32)
    mu = jnp.mean(xf, axis=-1, keepdims=True)
    var = jnp.mean(jnp.square(xf - mu), axis=-1, keepdims=True)
    y = (xf - mu) * lax.rsqrt(var + LN_EPS)
    if g is not None:
        y = y * g.astype(jnp.float32) + b.astype(jnp.float32)
    return y.astype(x.dtype)


def modulate(h, shift, scale):
    return h * (1.0 + scale[:, None, :]) + shift[:, None, :]


def diff_attention(q, k, v, lam, slopes, sub_g, lambda_init):
    B, S = q.shape[0], q.shape[1]
    nb = S // Q_BLOCK
    scale = HEAD_DIM ** -0.5
    kpos = jnp.arange(S)
    qb = q.reshape(B, nb, Q_BLOCK, N_HEADS, 2, HEAD_DIM).transpose(1, 0, 2, 3, 4, 5)

    def block(args):
        qblk, i = args
        qpos = i * Q_BLOCK + jnp.arange(Q_BLOCK)
        dist = jnp.abs(qpos[:, None] - kpos[None, :]).astype(jnp.float32)
        bias = -slopes[:, None, None] * dist[None]
        s = jnp.einsum('bqhmd,bkhmd->bhmqk', qblk, k,
                       preferred_element_type=jnp.float32) * scale + bias[None, :, None]
        p = jax.nn.softmax(s, axis=-1)
        a = p[:, :, 0] - lam * p[:, :, 1]
        return jnp.einsum('bhqk,bkhe->bqhe', a, v.astype(jnp.float32))

    o = lax.map(block, (qb, jnp.arange(nb)))
    o = o.transpose(1, 0, 2, 3, 4).reshape(B, S, N_HEADS, V_DIM)
    o = o * lax.rsqrt(jnp.mean(jnp.square(o), axis=-1, keepdims=True) + RMS_EPS)
    o = o * sub_g.astype(jnp.float32) * (1.0 - lambda_init)
    return o.reshape(B, S, ATTN_V).astype(q.dtype)


def multiscale_pool(u, w_pool, pool_scale):
    B, S = u.shape[0], u.shape[1]
    ug = u.astype(jnp.float32).reshape(B, S, N_POOL_GROUPS, POOL_GROUP_IN)
    csum = jnp.concatenate(
        [jnp.zeros((B, 1, N_POOL_GROUPS, POOL_GROUP_IN), jnp.float32), jnp.cumsum(ug, axis=1)], axis=1)
    t = jnp.arange(S)
    half = jnp.array([w // 2 for w in POOL_WINDOWS])
    lo = jnp.clip(t[:, None] - half[None, :], 0, S)
    hi = jnp.clip(t[:, None] + half[None, :], 0, S)
    g_idx = jnp.arange(N_POOL_GROUPS)[None, :]
    win_sum = csum[:, hi, g_idx] - csum[:, lo, g_idx]
    cnt = (hi - lo).astype(jnp.float32)[None, :, :, None]
    pooled = win_sum / cnt - ug
    y = jnp.einsum('bsgc,gce->bsge', pooled, w_pool.astype(jnp.float32))
    y = y.reshape(B, S, D_MODEL) * pool_scale.astype(jnp.float32)
    return y.astype(u.dtype)


def setup_inputs(seed: int = 0) -> dict:
    key = jax.random.key(seed)
    ks = jax.random.split(key, 20)
    n = jax.random.normal
    f32 = jnp.float32
    L, D = DEPTH, D_MODEL
    return {
        "x": n(ks[0], (BATCH, SEQ, D), f32),
        "c": n(ks[1], (BATCH, D), f32),
        "w_ada": n(ks[2], (L, D, N_MOD * D), f32) * (0.5 * D ** -0.5),
        "b_ada": n(ks[3], (L, N_MOD * D), f32) * 0.01,
        "w_in": n(ks[4], (L, D, IN_COLS), f32) * D ** -0.5,
        "lambda_q1": n(ks[5], (L, HEAD_DIM), f32) * 0.1,
        "lambda_k1": n(ks[6], (L, HEAD_DIM), f32) * 0.1,
        "lambda_q2": n(ks[7], (L, HEAD_DIM), f32) * 0.1,
        "lambda_k2": n(ks[8], (L, HEAD_DIM), f32) * 0.1,
        "sub_g": 1.0 + 0.02 * n(ks[9], (L, V_DIM), f32),
        "w_pool": n(ks[10], (L, N_POOL_GROUPS, POOL_GROUP_IN, POOL_GROUP_OUT), f32) * POOL_GROUP_IN ** -0.5,
        "pool_scale": 1.0 + 0.1 * n(ks[11], (L, D), f32),
        "w_out": n(ks[12], (L, D, D), f32) * (BETA * D ** -0.5),
        "ln1_g": 1.0 + 0.02 * n(ks[13], (L, D), f32),
        "ln1_b": 0.02 * n(ks[14], (L, D), f32),
        "w_ffn_in": n(ks[15], (L, D, 2 * D_FF), f32) * D ** -0.5,
        "w_ffn_out": n(ks[16], (L, D_FF, D), f32) * (BETA * D_FF ** -0.5),
        "ln2_g": 1.0 + 0.02 * n(ks[17], (L, D), f32),
        "ln2_b": 0.02 * n(ks[18], (L, D), f32),
    }


def reference(x, c, w_ada, b_ada, w_in, lambda_q1, lambda_k1, lambda_q2, lambda_k2, sub_g,
              w_pool, pool_scale, w_out, ln1_g, ln1_b, w_ffn_in, w_ffn_out, ln2_g, ln2_b):
    B, S, D = x.shape
    slopes = jnp.asarray(alibi_slopes(N_HEADS))
    c_act = jax.nn.silu(c)
    for l in range(DEPTH):
        lambda_init = 0.8 - 0.6 * math.exp(-0.3 * l)
        mod = c_act @ w_ada[l] + b_ada[l]
        sh1, sc1, g1, sh2, sc2, g2 = jnp.split(mod, N_MOD, axis=-1)

        h = modulate(layer_norm(x), sh1, sc1)
        proj = h @ w_in[l]
        q, k, v, u, gates = jnp.split(
            proj, np.cumsum([ATTN_QK, ATTN_QK, ATTN_V, POOL_IN]).tolist(), axis=-1)
        q = q.reshape(B, S, N_HEADS, 2, HEAD_DIM)
        k = k.reshape(B, S, N_HEADS, 2, HEAD_DIM)
        v = v.reshape(B, S, N_HEADS, V_DIM)
        lam = (jnp.exp(jnp.sum(lambda_q1[l].astype(jnp.float32) * lambda_k1[l].astype(jnp.float32)))
               - jnp.exp(jnp.sum(lambda_q2[l].astype(jnp.float32) * lambda_k2[l].astype(jnp.float32)))
               + lambda_init)
        a_out = diff_attention(q, k, v, lam, slopes, sub_g[l], lambda_init)
        p_out = multiscale_pool(u, w_pool[l], pool_scale[l])
        ga, gp = jnp.split(gates, N_BRANCHES, axis=-1)
        mixed = jax.nn.sigmoid(ga) * a_out + jax.nn.sigmoid(gp) * p_out
        y = mixed @ w_out[l]
        x = layer_norm(ALPHA * x + g1[:, None, :] * y, ln1_g[l], ln1_b[l])

        h = modulate(layer_norm(x), sh2, sc2)
        gt, up = jnp.split(h @ w_ffn_in[l], 2, axis=-1)
        y = (jax.nn.silu(gt) * up) @ w_ffn_out[l]
        x = layer_norm(ALPHA * x + g2[:, None, :] * y, ln2_g[l], ln2_b[l])
    return x
```

```python
import functools
import math

import jax
import jax.numpy as jnp
import numpy as np
from jax import lax
from jax.experimental import pallas as pl
from jax.experimental.pallas import tpu as pltpu

F32 = jnp.float32
BF16 = jnp.bfloat16

N_HEADS = 8
HEAD_DIM = 64
V_DIM = 2 * HEAD_DIM
POOL_WINDOWS = (2, 4, 8, 16)
POOL_GROUP_IN = 128
N_MOD = 6
DEPTH = 1
ALPHA = (2 * DEPTH) ** 0.25
LN_EPS = 1e-5
RMS_EPS = 1e-5
LOG2E = 1.4426950408889634

V7X_VMEM_LIMIT_BYTES = 56 * 1024 * 1024

TOKEN_TILE = 512
Q_TILE = 256
POOL_PAD = 8
FF_CHUNKS = ((0, 1024), (1024, 1024), (2048, 768))


def _params(n_grid_dims):
    return pltpu.CompilerParams(
        dimension_semantics=("arbitrary",) * n_grid_dims,
        vmem_limit_bytes=V7X_VMEM_LIMIT_BYTES)


def _resident(shape):
    nd = len(shape)
    return pl.BlockSpec(shape, lambda *_: (0,) * nd, pipeline_mode=pl.Buffered(1))


def _dot(a, b):
    return jnp.dot(a, b, preferred_element_type=F32)


def _dot_nt(a, b):
    return lax.dot_general(a, b, (((1,), (1,)), ((), ())), preferred_element_type=F32)


def _normalize(x):
    mu = jnp.mean(x, axis=-1, keepdims=True)
    xc = x - mu
    var = jnp.mean(xc * xc, axis=-1, keepdims=True)
    return xc * lax.rsqrt(var + LN_EPS)


def _mod_kernel(c_ref, w_ref, b_ref, o_ref):
    c = c_ref[...]
    ca = c * jax.nn.sigmoid(c)
    ca_hi = ca.astype(BF16)
    ca_lo = (ca - ca_hi.astype(F32)).astype(BF16)
    w = w_ref[...]
    w_hi = w.astype(BF16)
    w_lo = (w - w_hi.astype(F32)).astype(BF16)
    acc = _dot(ca_hi, w_hi) + _dot(ca_lo, w_hi) + _dot(ca_hi, w_lo)
    o_ref[...] = acc + b_ref[...]


def _mod(c, w_ada, b_ada):
    B, D = c.shape
    N = w_ada.shape[1]
    tn = 1536
    return pl.pallas_call(
        _mod_kernel,
        grid=(N // tn,),
        in_specs=[pl.BlockSpec((B, D), lambda j: (0, 0)),
                  pl.BlockSpec((D, tn), lambda j: (0, j)),
                  pl.BlockSpec((1, tn), lambda j: (0, j))],
        out_specs=pl.BlockSpec((B, tn), lambda j: (0, j)),
        out_shape=jax.ShapeDtypeStruct((B, N), F32),
        compiler_params=_params(1),
        name="mod",
    )(c, w_ada, b_ada.reshape(1, N))


def _in_proj_kernel(x_ref, sh_ref, sc_ref, wqT_ref, wk_ref, wvT_ref, wu_ref, wga_ref, wgp_ref,
                    qT_ref, k_ref, vT_ref, u_ref, ga_ref, gp_ref):
    h = _normalize(x_ref[0]) * (1.0 + sc_ref[0]) + sh_ref[0]
    hb = h.astype(BF16)
    k_ref[0] = _dot(hb, wk_ref[...]).astype(BF16)
    u_ref[0] = _dot(hb, wu_ref[...])
    ga_ref[0] = _dot(hb, wga_ref[...]).astype(BF16)
    gp_ref[0] = _dot(hb, wgp_ref[...]).astype(BF16)
    qT = _dot_nt(wqT_ref[...], hb).astype(BF16)
    for j in range(TOKEN_TILE // Q_TILE):
        qT_ref[0, j] = qT[:, j * Q_TILE:(j + 1) * Q_TILE]
    vT_ref[0] = _dot_nt(wvT_ref[...], hb).astype(BF16)


def _in_proj(x, sh1, sc1, wqT, wk, wvT, wu, wga, wgp):
    B, S, D = x.shape
    tm = TOKEN_TILE
    nq = tm // Q_TILE
    du = wu.shape[1]
    tok = lambda w: pl.BlockSpec((1, tm, w), lambda b, i: (b, i, 0))
    vec = pl.BlockSpec((1, 1, D), lambda b, i: (b, 0, 0))
    return pl.pallas_call(
        _in_proj_kernel,
        grid=(B, S // tm),
        in_specs=[tok(D), vec, vec,
                  _resident(wqT.shape), _resident(wk.shape), _resident(wvT.shape),
                  _resident(wu.shape), _resident(wga.shape), _resident(wgp.shape)],
        out_specs=[pl.BlockSpec((1, nq, D, Q_TILE), lambda b, i: (b, i, 0, 0)),
                   tok(D),
                   pl.BlockSpec((1, D, tm), lambda b, i: (b, 0, i)),
                   tok(du), tok(D), tok(D)],
        out_shape=[jax.ShapeDtypeStruct((B, S // Q_TILE, D, Q_TILE), BF16),
                   jax.ShapeDtypeStruct((B, S, D), BF16),
                   jax.ShapeDtypeStruct((B, D, S), BF16),
                   jax.ShapeDtypeStruct((B, S, du), F32),
                   jax.ShapeDtypeStruct((B, S, D), BF16),
                   jax.ShapeDtypeStruct((B, S, D), BF16)],
        compiler_params=_params(2),
        name="in_proj",
    )(x, sh1, sc1, wqT, wk, wvT, wu, wga, wgp)


def _attn_kernel(hc_ref, lq1_ref, lk1_ref, lq2_ref, lk2_ref, subg_ref, qT_ref, k_ref, vT_ref,
                 o_ref, dist_ref, vext_ref, *, seq, lambda_init):
    b = pl.program_id(0)
    h = pl.program_id(1)
    n_q = seq // Q_TILE

    @pl.when((b == 0) & (h == 0))
    def _():
        r = lax.broadcasted_iota(jnp.int32, dist_ref.shape, 0)
        c = lax.broadcasted_iota(jnp.int32, dist_ref.shape, 1)
        dist_ref[...] = jnp.abs(r - c - (seq - Q_TILE)).astype(F32)

    vext_ref[0:V_DIM, :] = vT_ref[0]
    vext_ref[V_DIM:, :] = jnp.ones((vext_ref.shape[0] - V_DIM, seq), BF16)

    lam = (jnp.exp(jnp.sum(lq1_ref[...] * lk1_ref[...], keepdims=True))
           - jnp.exp(jnp.sum(lq2_ref[...] * lk2_ref[...], keepdims=True))
           + lambda_init)

    q_scale = hc_ref[0, h]
    c_exp = hc_ref[1, h]

    k = k_ref[0]
    row = lax.broadcasted_iota(jnp.int32, (2 * HEAD_DIM, Q_TILE), 0)
    map_rows = (row < HEAD_DIM, row >= HEAD_DIM)

    def q_block(qi, carry):
        qs = qT_ref[0, qi].astype(F32) * q_scale
        off = pl.multiple_of((seq - Q_TILE) - qi * Q_TILE, Q_TILE)
        d = dist_ref[pl.ds(off, seq), :]
        outs = []
        for rows in map_rows:
            qm = jnp.where(rows, qs, 0.0).astype(BF16)
            z = _dot(k, qm) - d
            m = jnp.max(z, axis=0, keepdims=True)
            p = jnp.exp2((z - m) * c_exp).astype(BF16)
            acc = _dot(vext_ref[...], p)
            outs.append(acc[0:V_DIM] / acc[V_DIM:V_DIM + 1])
        a = outs[0] - lam * outs[1]
        a = a * lax.rsqrt(jnp.mean(a * a, axis=0, keepdims=True) + RMS_EPS)
        aT = a.T * (subg_ref[...] * (1.0 - lambda_init))
        o_ref[0, pl.ds(pl.multiple_of(qi * Q_TILE, Q_TILE), Q_TILE), :] = aT.astype(BF16)
        return carry

    lax.fori_loop(0, n_q, q_block, 0)


def _attn(qT, k, vT, lq1, lk1, lq2, lk2, sub_g, lambda_init):
    B, S, D = k.shape
    n_q = S // Q_TILE
    lam_spec = pl.BlockSpec((1, HEAD_DIM), lambda b, h: (0, 0))
    kern = functools.partial(_attn_kernel, seq=S, lambda_init=lambda_init)
    slopes = 2.0 ** -(np.arange(N_HEADS, dtype=np.float64) + 1.0)
    head_consts = jnp.asarray(np.stack([HEAD_DIM ** -0.5 / slopes, slopes * LOG2E]), F32)
    return pl.pallas_call(
        kern,
        grid=(B, N_HEADS),
        in_specs=[pl.BlockSpec(memory_space=pltpu.SMEM),
                  lam_spec, lam_spec, lam_spec, lam_spec,
                  pl.BlockSpec((1, V_DIM), lambda b, h: (0, 0)),
                  pl.BlockSpec((1, n_q, V_DIM, Q_TILE), lambda b, h: (b, 0, h, 0)),
                  pl.BlockSpec((1, S, V_DIM), lambda b, h: (b, 0, h)),
                  pl.BlockSpec((1, V_DIM, S), lambda b, h: (b, h, 0))],
        out_specs=pl.BlockSpec((1, S, V_DIM), lambda b, h: (b, 0, h)),
        out_shape=jax.ShapeDtypeStruct((B, S, D), BF16),
        scratch_shapes=[pltpu.VMEM((2 * S - Q_TILE, Q_TILE), F32),
                        pltpu.VMEM((V_DIM + 16, S), BF16)],
        compiler_params=_params(2),
        name="attn",
    )(head_consts, lq1, lk1, lq2, lk2, sub_g, qT, k, vT)


def _pool_kernel(u_ref, o_ref, *, seq):
    n = seq + 2 * POOL_PAD
    t = lax.broadcasted_iota(jnp.int32, (seq, 1), 0)
    zeros = jnp.zeros((POOL_PAD, POOL_GROUP_IN), F32)
    for g, w in enumerate(POOL_WINDOWS):
        half = w // 2
        ug = u_ref[0, :, g * POOL_GROUP_IN:(g + 1) * POOL_GROUP_IN]
        x = jnp.concatenate([zeros, ug, zeros], axis=0)
        win = x + pltpu.roll(x, 1, 0)
        step = 1
        while step < half:
            win = pltpu.roll(win, step, 0) + pltpu.roll(win, n - step, 0)
            step *= 2
        win = win[POOL_PAD:POOL_PAD + seq]
        cnt = (jnp.minimum(t + half, seq) - jnp.maximum(t - half, 0)).astype(F32)
        o_ref[0, :, g * POOL_GROUP_IN:(g + 1) * POOL_GROUP_IN] = (win / cnt - ug).astype(BF16)


def _pool(u):
    B, S, C = u.shape
    return pl.pallas_call(
        functools.partial(_pool_kernel, seq=S),
        grid=(B,),
        in_specs=[pl.BlockSpec((1, S, C), lambda b: (b, 0, 0))],
        out_specs=pl.BlockSpec((1, S, C), lambda b: (b, 0, 0)),
        out_shape=jax.ShapeDtypeStruct((B, S, C), BF16),
        compiler_params=_params(1),
        name="pool",
    )(u)


def _mix_out_kernel(a_ref, pooled_ref, ga_ref, gp_ref, x_ref, g1_ref, sh2_ref, sc2_ref,
                    wpool_ref, pscale_ref, wout_ref, ln1g_ref, ln1b_ref, x1_ref, h2_ref):
    pooled = pooled_ref[0]
    p_out = jnp.concatenate(
        [_dot(pooled[:, g * POOL_GROUP_IN:(g + 1) * POOL_GROUP_IN], wpool_ref[g])
         for g in range(len(POOL_WINDOWS))], axis=-1) * pscale_ref[...]
    mixed = (jax.nn.sigmoid(ga_ref[0].astype(F32)) * a_ref[0].astype(F32)
             + jax.nn.sigmoid(gp_ref[0].astype(F32)) * p_out)
    y = _dot(mixed.astype(BF16), wout_ref[...])
    x1 = _normalize(ALPHA * x_ref[0] + g1_ref[0] * y) * ln1g_ref[...] + ln1b_ref[...]
    x1_ref[0] = x1
    h2_ref[0] = (_normalize(x1) * (1.0 + sc2_ref[0]) + sh2_ref[0]).astype(BF16)


def _mix_out(a, pooled, ga, gp, x, g1, sh2, sc2, wpool, pscale, wout, ln1g, ln1b):
    B, S, D = x.shape
    tm = TOKEN_TILE
    tok = lambda w: pl.BlockSpec((1, tm, w), lambda b, i: (b, i, 0))
    vec = pl.BlockSpec((1, 1, D), lambda b, i: (b, 0, 0))
    return pl.pallas_call(
        _mix_out_kernel,
        grid=(B, S // tm),
        in_specs=[tok(D), tok(pooled.shape[-1]), tok(D), tok(D), tok(D), vec, vec, vec,
                  _resident(wpool.shape), _resident(pscale.shape), _resident(wout.shape),
                  _resident(ln1g.shape), _resident(ln1b.shape)],
        out_specs=[tok(D), tok(D)],
        out_shape=[jax.ShapeDtypeStruct((B, S, D), F32), jax.ShapeDtypeStruct((B, S, D), BF16)],
        compiler_params=_params(2),
        name="mix_out",
    )(a, pooled, ga, gp, x, g1, sh2, sc2, wpool, pscale, wout, ln1g, ln1b)


def _ffn_kernel(h_ref, x_ref, g2_ref, wg_ref, wu_ref, wo_ref, ln2g_ref, ln2b_ref, o_ref):
    h = h_ref[0]
    y = None
    for start, width in FF_CHUNKS:
        gt = _dot(h, wg_ref[:, start:start + width])
        up = _dot(h, wu_ref[:, start:start + width])
        act = (gt * jax.nn.sigmoid(gt) * up).astype(BF16)
        part = _dot(act, wo_ref[start:start + width, :])
        y = part if y is None else y + part
    xn = ALPHA * x_ref[0] + g2_ref[0] * y
    o_ref[0] = _normalize(xn) * ln2g_ref[...] + ln2b_ref[...]


def _ffn(h2, x1, g2, wg, wu, wo, ln2g, ln2b):
    B, S, D = x1.shape
    tm = TOKEN_TILE
    tok = pl.BlockSpec((1, tm, D), lambda b, i: (b, i, 0))
    vec = pl.BlockSpec((1, 1, D), lambda b, i: (b, 0, 0))
    return pl.pallas_call(
        _ffn_kernel,
        grid=(B, S // tm),
        in_specs=[tok, tok, vec, _resident(wg.shape), _resident(wu.shape), _resident(wo.shape),
                  _resident(ln2g.shape), _resident(ln2b.shape)],
        out_specs=tok,
        out_shape=jax.ShapeDtypeStruct((B, S, D), F32),
        compiler_params=_params(2),
        name="ffn",
    )(h2, x1, g2, wg, wu, wo, ln2g, ln2b)


def kernel(x, c, w_ada, b_ada, w_in, lambda_q1, lambda_k1, lambda_q2, lambda_k2, sub_g, w_pool,
           pool_scale, w_out, ln1_g, ln1_b, w_ffn_in, w_ffn_out, ln2_g, ln2_b):
    B, S, D = x.shape
    assert w_ada.shape[0] == DEPTH == 1
    assert S % TOKEN_TILE == 0 and TOKEN_TILE % Q_TILE == 0
    d_ff = w_ffn_out.shape[1]
    assert sum(w for _, w in FF_CHUNKS) == d_ff
    pool_in = len(POOL_WINDOWS) * POOL_GROUP_IN
    qk = N_HEADS * 2 * HEAD_DIM
    lambda_init = 0.8 - 0.6 * math.exp(-0.3 * 0)

    mod = _mod(c, w_ada[0], b_ada[0])
    sh1, sc1, g1, sh2, sc2, g2 = [m.reshape(B, 1, D) for m in jnp.split(mod, N_MOD, axis=-1)]

    w = w_in[0].astype(BF16)
    bounds = np.cumsum([qk, qk, D, pool_in, D]).tolist()
    wq, wk, wv, wu, wga, wgp = jnp.split(w, bounds, axis=-1)
    qT, k, vT, u, ga, gp = _in_proj(x, sh1, sc1, wq.T, wk, wv.T, wu, wga, wgp)

    row = lambda v: v.reshape(1, -1)
    a = _attn(qT, k, vT, row(lambda_q1[0]), row(lambda_k1[0]), row(lambda_q2[0]),
              row(lambda_k2[0]), row(sub_g[0]), lambda_init)
    pooled = _pool(u)
    x1, h2 = _mix_out(a, pooled, ga, gp, x, g1, sh2, sc2, w_pool[0].astype(BF16),
                      row(pool_scale[0]), w_out[0].astype(BF16), row(ln1_g[0]), row(ln1_b[0]))
    wf = w_ffn_in[0].astype(BF16)
    return _ffn(h2, x1, g2, wf[:, :d_ff], wf[:, d_ff:], w_ffn_out[0].astype(BF16),
                row(ln2_g[0]), row(ln2_b[0]))
```

```python
import functools
import math

import jax
import jax.numpy as jnp
import numpy as np
from jax import lax
from jax.experimental import pallas as pl
from jax.experimental.pallas import tpu as pltpu

F32 = jnp.float32
BF16 = jnp.bfloat16

N_HEADS = 8
HEAD_DIM = 64
V_DIM = 2 * HEAD_DIM
POOL_WINDOWS = (2, 4, 8, 16)
POOL_GROUP_IN = 128
N_MOD = 6
DEPTH = 1
ALPHA = (2 * DEPTH) ** 0.25
LN_EPS = 1e-5
RMS_EPS = 1e-5
LOG2E = 1.4426950408889634

V7X_VMEM_LIMIT_BYTES = 56 * 1024 * 1024

TOKEN_TILE = 512
Q_TILE = 256
KEY_CHUNK = 256
N_SLOTS = 3
POOL_PAD = 8
FF_CHUNKS = ((0, 1024), (1024, 1024), (2048, 768))


def _params(n_grid_dims, flags=None):
    return pltpu.CompilerParams(
        dimension_semantics=("arbitrary",) * n_grid_dims,
        vmem_limit_bytes=V7X_VMEM_LIMIT_BYTES, flags=flags)


def _resident(shape):
    nd = len(shape)
    return pl.BlockSpec(shape, lambda *_: (0,) * nd, pipeline_mode=pl.Buffered(1))


def _dot(a, b):
    return jnp.dot(a, b, preferred_element_type=F32)


def _dot_nt(a, b):
    return lax.dot_general(a, b, (((1,), (1,)), ((), ())), preferred_element_type=F32)


def _normalize(x):
    mu = jnp.mean(x, axis=-1, keepdims=True)
    xc = x - mu
    var = jnp.mean(xc * xc, axis=-1, keepdims=True)
    return xc * lax.rsqrt(var + LN_EPS)


def _mod_kernel(c_ref, w_ref, b_ref, o_ref):
    c = c_ref[...]
    ca = c * jax.nn.sigmoid(c)
    ca_hi = ca.astype(BF16)
    ca_lo = (ca - ca_hi.astype(F32)).astype(BF16)
    w = w_ref[...]
    w_hi = w.astype(BF16)
    w_lo = (w - w_hi.astype(F32)).astype(BF16)
    acc = _dot(ca_hi, w_hi) + _dot(ca_lo, w_hi) + _dot(ca_hi, w_lo)
    o_ref[...] = acc + b_ref[...]


def _mod(c, w_ada, b_ada):
    B, D = c.shape
    N = w_ada.shape[1]
    tn = 1536
    return pl.pallas_call(
        _mod_kernel,
        grid=(N // tn,),
        in_specs=[pl.BlockSpec((B, D), lambda j: (0, 0)),
                  pl.BlockSpec((D, tn), lambda j: (0, j)),
                  pl.BlockSpec((1, tn), lambda j: (0, j))],
        out_specs=pl.BlockSpec((B, tn), lambda j: (0, j)),
        out_shape=jax.ShapeDtypeStruct((B, N), F32),
        compiler_params=_params(1),
        name="mod",
    )(c, w_ada, b_ada.reshape(1, N))


def _in_proj_kernel(x_ref, sh_ref, sc_ref, wqT_ref, wk_ref, wvT_ref, wu_ref, wga_ref, wgp_ref,
                    qT_ref, k_ref, vT_ref, u_ref, ga_ref, gp_ref):
    h = _normalize(x_ref[0]) * (1.0 + sc_ref[0]) + sh_ref[0]
    hb = h.astype(BF16)
    k_ref[0] = _dot(hb, wk_ref[...]).astype(BF16)
    u_ref[0] = _dot(hb, wu_ref[...])
    ga_ref[0] = _dot(hb, wga_ref[...]).astype(BF16)
    gp_ref[0] = _dot(hb, wgp_ref[...]).astype(BF16)
    qT = _dot_nt(wqT_ref[...], hb).astype(BF16)
    for j in range(TOKEN_TILE // Q_TILE):
        qT_ref[0, j] = qT[:, j * Q_TILE:(j + 1) * Q_TILE]
    vT_ref[0] = _dot_nt(wvT_ref[...], hb).astype(BF16)


def _in_proj(x, sh1, sc1, wqT, wk, wvT, wu, wga, wgp):
    B, S, D = x.shape
    tm = TOKEN_TILE
    nq = tm // Q_TILE
    du = wu.shape[1]
    tok = lambda w: pl.BlockSpec((1, tm, w), lambda b, i: (b, i, 0))
    vec = pl.BlockSpec((1, 1, D), lambda b, i: (b, 0, 0))
    return pl.pallas_call(
        _in_proj_kernel,
        grid=(B, S // tm),
        in_specs=[tok(D), vec, vec,
                  _resident(wqT.shape), _resident(wk.shape), _resident(wvT.shape),
                  _resident(wu.shape), _resident(wga.shape), _resident(wgp.shape)],
        out_specs=[pl.BlockSpec((1, nq, D, Q_TILE), lambda b, i: (b, i, 0, 0)),
                   tok(D),
                   pl.BlockSpec((1, D, tm), lambda b, i: (b, 0, i)),
                   tok(du), tok(D), tok(D)],
        out_shape=[jax.ShapeDtypeStruct((B, S // Q_TILE, D, Q_TILE), BF16),
                   jax.ShapeDtypeStruct((B, S, D), BF16),
                   jax.ShapeDtypeStruct((B, D, S), BF16),
                   jax.ShapeDtypeStruct((B, S, du), F32),
                   jax.ShapeDtypeStruct((B, S, D), BF16),
                   jax.ShapeDtypeStruct((B, S, D), BF16)],
        compiler_params=_params(2),
        name="in_proj",
    )(x, sh1, sc1, wqT, wk, wvT, wu, wga, wgp)


def _attn_kernel(hc_ref, lq1_ref, lk1_ref, lq2_ref, lk2_ref, subg_ref, qT_ref, k_ref, vT_ref,
                 o_ref, kpos_ref, diag_ref, vext_ref, *slots, seq, lambda_init):
    b = pl.program_id(0)
    h = pl.program_id(1)
    n_q = seq // Q_TILE
    assert KEY_CHUNK == Q_TILE
    z_refs, p_refs, m_refs = (slots[i * N_SLOTS:(i + 1) * N_SLOTS] for i in range(3))

    @pl.when((b == 0) & (h == 0))
    def _():
        j = lax.broadcasted_iota(jnp.int32, kpos_ref.shape, 0)
        l = lax.broadcasted_iota(jnp.int32, kpos_ref.shape, 1)
        kpos = jnp.where(l < 2, 1, jnp.where(l == 2, j - j % Q_TILE, jnp.where(l == 3, j % Q_TILE, 0)))
        kpos_ref[...] = kpos.astype(F32).astype(BF16)
        r = lax.broadcasted_iota(jnp.int32, diag_ref.shape, 0)
        c = lax.broadcasted_iota(jnp.int32, diag_ref.shape, 1)
        diag_ref[...] = jnp.abs(r - c % Q_TILE).astype(F32)

    vext_ref[0:V_DIM, :] = vT_ref[0]
    vext_ref[V_DIM:, :] = jnp.ones((vext_ref.shape[0] - V_DIM, seq), BF16)

    lam = (jnp.exp(jnp.sum(lq1_ref[...] * lk1_ref[...], keepdims=True))
           - jnp.exp(jnp.sum(lq2_ref[...] * lk2_ref[...], keepdims=True))
           + lambda_init)
    out_gain = subg_ref[...] * (1.0 - lambda_init)

    q_scale = hc_ref[0, h]
    c_exp = jnp.full((1, 2 * Q_TILE), hc_ref[1, h], F32).astype(BF16)

    row = lax.broadcasted_iota(jnp.int32, (2 * HEAD_DIM, Q_TILE), 0)
    pos_row = lax.broadcasted_iota(jnp.int32, (2 * HEAD_DIM, 2 * Q_TILE), 0)
    pos_col = lax.broadcasted_iota(jnp.int32, (2 * HEAD_DIM, 2 * Q_TILE), 1) % Q_TILE
    n_chunks = seq // KEY_CHUNK

    def scores_probs(qi, pi):
        if qi is not None:
            qs = qT_ref[0, qi].astype(F32) * q_scale
            qcat = jnp.concatenate([jnp.where(row < HEAD_DIM, qs, 0.0),
                                    jnp.where(row >= HEAD_DIM, qs, 0.0)], axis=1).astype(BF16)
            qpos = jnp.where(pos_row == 0, -qi * Q_TILE,
                             jnp.where(pos_row == 1, -pos_col,
                                       jnp.where(pos_row < 4, 1, 0))).astype(F32)
            q_before = jnp.concatenate([qcat, qpos.astype(BF16)], axis=0)
            q_after = jnp.concatenate([qcat, (-qpos).astype(BF16)], axis=0)
            z_ref = z_refs[qi % N_SLOTS]
        if pi is not None:
            zp_ref, p_ref = z_refs[pi % N_SLOTS], p_refs[pi % N_SLOTS]
            mp = m_refs[pi % N_SLOTS][...]
        m = None
        for c in range(n_chunks):
            rows = slice(c * KEY_CHUNK, (c + 1) * KEY_CHUNK)
            if qi is not None:
                if c == qi:
                    z = _dot(k_ref[0, rows, :], qcat) - diag_ref[...]
                else:
                    kx = jnp.concatenate([k_ref[0, rows, :], kpos_ref[rows, :]], axis=1)
                    z = _dot(kx, q_before if c < qi else q_after)
                z_ref[rows, :] = z
                mc = jnp.max(z, axis=0, keepdims=True)
                m = mc if m is None else jnp.maximum(m, mc)
            if pi is not None:
                p_ref[rows, :] = jnp.exp2((zp_ref[rows, :] - mp).astype(BF16) * c_exp)
        if qi is not None:
            m_refs[qi % N_SLOTS][...] = m

    def values(qi):
        acc = _dot(vext_ref[...], p_refs[qi % N_SLOTS][...])
        o1 = acc[0:V_DIM, 0:Q_TILE] / acc[V_DIM:V_DIM + 1, 0:Q_TILE]
        o2 = acc[0:V_DIM, Q_TILE:] / acc[V_DIM:V_DIM + 1, Q_TILE:]
        a = o1 - lam * o2
        a = a * lax.rsqrt(jnp.mean(a * a, axis=0, keepdims=True) + RMS_EPS)
        o_ref[0, qi * Q_TILE:(qi + 1) * Q_TILE, :] = (a.T * out_gain).astype(BF16)

    for t in range(n_q + 2):
        if t >= 2:
            values(t - 2)
        scores_probs(t if t < n_q else None, t - 1 if 1 <= t <= n_q else None)


def _attn(qT, k, vT, lq1, lk1, lq2, lk2, sub_g, lambda_init):
    B, S, D = k.shape
    n_q = S // Q_TILE
    lam_spec = pl.BlockSpec((1, HEAD_DIM), lambda b, h: (0, 0))
    kern = functools.partial(_attn_kernel, seq=S, lambda_init=lambda_init)
    slopes = 2.0 ** -(np.arange(N_HEADS, dtype=np.float64) + 1.0)
    head_consts = jnp.asarray(np.stack([HEAD_DIM ** -0.5 / slopes, slopes * LOG2E]), F32)
    return pl.pallas_call(
        kern,
        grid=(B, N_HEADS),
        in_specs=[pl.BlockSpec(memory_space=pltpu.SMEM),
                  lam_spec, lam_spec, lam_spec, lam_spec,
                  pl.BlockSpec((1, V_DIM), lambda b, h: (0, 0)),
                  pl.BlockSpec((1, n_q, V_DIM, Q_TILE), lambda b, h: (b, 0, h, 0)),
                  pl.BlockSpec((1, S, V_DIM), lambda b, h: (b, 0, h)),
                  pl.BlockSpec((1, V_DIM, S), lambda b, h: (b, h, 0))],
        out_specs=pl.BlockSpec((1, S, V_DIM), lambda b, h: (b, 0, h)),
        out_shape=jax.ShapeDtypeStruct((B, S, D), BF16),
        scratch_shapes=[pltpu.VMEM((S, 2 * HEAD_DIM), BF16),
                        pltpu.VMEM((Q_TILE, 2 * Q_TILE), F32),
                        pltpu.VMEM((V_DIM + 16, S), BF16),
                        *[pltpu.VMEM((S, 2 * Q_TILE), F32)] * N_SLOTS,
                        *[pltpu.VMEM((S, 2 * Q_TILE), BF16)] * N_SLOTS,
                        *[pltpu.VMEM((1, 2 * Q_TILE), F32)] * N_SLOTS],
        compiler_params=_params(2),
        name="attn",
    )(head_consts, lq1, lk1, lq2, lk2, sub_g, qT, k, vT)


def _pool_kernel(u_ref, o_ref, *, seq):
    n = seq + 2 * POOL_PAD
    t = lax.broadcasted_iota(jnp.int32, (seq, 1), 0)
    zeros = jnp.zeros((POOL_PAD, POOL_GROUP_IN), F32)
    for g, w in enumerate(POOL_WINDOWS):
        half = w // 2
        ug = u_ref[0, :, g * POOL_GROUP_IN:(g + 1) * POOL_GROUP_IN]
        x = jnp.concatenate([zeros, ug, zeros], axis=0)
        win = x + pltpu.roll(x, 1, 0)
        step = 1
        while step < half:
            win = pltpu.roll(win, step, 0) + pltpu.roll(win, n - step, 0)
            step *= 2
        win = win[POOL_PAD:POOL_PAD + seq]
        cnt = (jnp.minimum(t + half, seq) - jnp.maximum(t - half, 0)).astype(F32)
        o_ref[0, :, g * POOL_GROUP_IN:(g + 1) * POOL_GROUP_IN] = (win / cnt - ug).astype(BF16)


def _pool(u):
    B, S, C = u.shape
    return pl.pallas_call(
        functools.partial(_pool_kernel, seq=S),
        grid=(B,),
        in_specs=[pl.BlockSpec((1, S, C), lambda b: (b, 0, 0))],
        out_specs=pl.BlockSpec((1, S, C), lambda b: (b, 0, 0)),
        out_shape=jax.ShapeDtypeStruct((B, S, C), BF16),
        compiler_params=_params(1),
        name="pool",
    )(u)


def _mix_ffn_kernel(a_ref, pooled_ref, ga_ref, gp_ref, x_ref, g1_ref, sh2_ref, sc2_ref, g2_ref,
                    wpool_ref, pscale_ref, wout_ref, ln1g_ref, ln1b_ref,
                    wg_ref, wu_ref, wo_ref, ln2g_ref, ln2b_ref, o_ref):
    pooled = pooled_ref[0]
    p_out = jnp.concatenate(
        [_dot(pooled[:, g * POOL_GROUP_IN:(g + 1) * POOL_GROUP_IN], wpool_ref[g])
         for g in range(len(POOL_WINDOWS))], axis=-1) * pscale_ref[...]
    mixed = (jax.nn.sigmoid(ga_ref[0].astype(F32)) * a_ref[0].astype(F32)
             + jax.nn.sigmoid(gp_ref[0].astype(F32)) * p_out)
    y = _dot(mixed.astype(BF16), wout_ref[...])
    x1 = _normalize(ALPHA * x_ref[0] + g1_ref[0] * y) * ln1g_ref[...] + ln1b_ref[...]
    h = (_normalize(x1) * (1.0 + sc2_ref[0]) + sh2_ref[0]).astype(BF16)
    y = None
    for start, width in FF_CHUNKS:
        gt = _dot(h, wg_ref[:, start:start + width])
        up = _dot(h, wu_ref[:, start:start + width])
        act = (gt * jax.nn.sigmoid(gt) * up).astype(BF16)
        part = _dot(act, wo_ref[start:start + width, :])
        y = part if y is None else y + part
    o_ref[0] = _normalize(ALPHA * x1 + g2_ref[0] * y) * ln2g_ref[...] + ln2b_ref[...]


def _mix_ffn(a, pooled, ga, gp, x, g1, sh2, sc2, g2, *weights):
    B, S, D = x.shape
    tm = TOKEN_TILE
    tok = lambda w: pl.BlockSpec((1, tm, w), lambda b, i: (b, i, 0))
    vec = pl.BlockSpec((1, 1, D), lambda b, i: (b, 0, 0))
    return pl.pallas_call(
        _mix_ffn_kernel,
        grid=(B, S // tm),
        in_specs=[tok(D), tok(pooled.shape[-1]), tok(D), tok(D), tok(D), vec, vec, vec, vec,
                  *[_resident(w.shape) for w in weights]],
        out_specs=tok(D),
        out_shape=jax.ShapeDtypeStruct((B, S, D), F32),
        compiler_params=_params(2),
        name="mix_ffn",
    )(a, pooled, ga, gp, x, g1, sh2, sc2, g2, *weights)


def kernel(x, c, w_ada, b_ada, w_in, lambda_q1, lambda_k1, lambda_q2, lambda_k2, sub_g, w_pool,
           pool_scale, w_out, ln1_g, ln1_b, w_ffn_in, w_ffn_out, ln2_g, ln2_b):
    B, S, D = x.shape
    assert w_ada.shape[0] == DEPTH == 1
    assert S % TOKEN_TILE == 0 and TOKEN_TILE % Q_TILE == 0
    d_ff = w_ffn_out.shape[1]
    assert sum(w for _, w in FF_CHUNKS) == d_ff
    pool_in = len(POOL_WINDOWS) * POOL_GROUP_IN
    qk = N_HEADS * 2 * HEAD_DIM
    lambda_init = 0.8 - 0.6 * math.exp(-0.3 * 0)

    mod = _mod(c, w_ada[0], b_ada[0])
    sh1, sc1, g1, sh2, sc2, g2 = [m.reshape(B, 1, D) for m in jnp.split(mod, N_MOD, axis=-1)]

    w = w_in[0].astype(BF16)
    bounds = np.cumsum([qk, qk, D, pool_in, D]).tolist()
    wq, wk, wv, wu, wga, wgp = jnp.split(w, bounds, axis=-1)
    qT, k, vT, u, ga, gp = _in_proj(x, sh1, sc1, wq.T, wk, wv.T, wu, wga, wgp)

    row = lambda v: v.reshape(1, -1)
    a = _attn(qT, k, vT, row(lambda_q1[0]), row(lambda_k1[0]), row(lambda_q2[0]),
              row(lambda_k2[0]), row(sub_g[0]), lambda_init)
    pooled = _pool(u)
    wf = w_ffn_in[0].astype(BF16)
    return _mix_ffn(a, pooled, ga, gp, x, g1, sh2, sc2, g2,
                    w_pool[0].astype(BF16), row(pool_scale[0]), w_out[0].astype(BF16),
                    row(ln1_g[0]), row(ln1_b[0]),
                    wf[:, :d_ff], wf[:, d_ff:], w_ffn_out[0].astype(BF16),
                    row(ln2_g[0]), row(ln2_b[0]))
```

```python
import functools
import math

import jax
import jax.numpy as jnp
import numpy as np
from jax import lax
from jax.experimental import pallas as pl
from jax.experimental.pallas import tpu as pltpu

F32 = jnp.float32
BF16 = jnp.bfloat16

N_HEADS = 8
HEAD_DIM = 64
V_DIM = 2 * HEAD_DIM
POOL_WINDOWS = (2, 4, 8, 16)
POOL_GROUP_IN = 128
N_MOD = 6
DEPTH = 1
ALPHA = (2 * DEPTH) ** 0.25
LN_EPS = 1e-5
RMS_EPS = 1e-5
LOG2E = 1.4426950408889634

V7X_VMEM_LIMIT_BYTES = 56 * 1024 * 1024

TOKEN_TILE = 512
Q_TILE = 256
KEY_CHUNK = 256
N_SLOTS = 3
HEADS_PER_STEP = 1
POOL_PAD = 8
FF_CHUNKS = ((0, 1024), (1024, 1024), (2048, 768))


def _params(n_grid_dims):
    return pltpu.CompilerParams(
        dimension_semantics=("arbitrary",) * n_grid_dims,
        vmem_limit_bytes=V7X_VMEM_LIMIT_BYTES)


def _resident(shape):
    nd = len(shape)
    return pl.BlockSpec(shape, lambda *_: (0,) * nd, pipeline_mode=pl.Buffered(1))


def _dot(a, b):
    return jnp.dot(a, b, preferred_element_type=F32)


def _dot_nt(a, b):
    return lax.dot_general(a, b, (((1,), (1,)), ((), ())), preferred_element_type=F32)


def _normalize(x):
    mu = jnp.mean(x, axis=-1, keepdims=True)
    xc = x - mu
    var = jnp.mean(xc * xc, axis=-1, keepdims=True)
    return xc * lax.rsqrt(var + LN_EPS)


def _mod_kernel(c_ref, w_ref, b_ref, o_ref):
    c = c_ref[...]
    ca = c * jax.nn.sigmoid(c)
    ca_hi = ca.astype(BF16)
    ca_lo = (ca - ca_hi.astype(F32)).astype(BF16)
    w = w_ref[...]
    w_hi = w.astype(BF16)
    w_lo = (w - w_hi.astype(F32)).astype(BF16)
    acc = _dot(ca_hi, w_hi) + _dot(ca_lo, w_hi) + _dot(ca_hi, w_lo)
    o_ref[...] = acc + b_ref[...]


def _mod(c, w_ada, b_ada):
    B, D = c.shape
    N = w_ada.shape[1]
    tn = 1536
    return pl.pallas_call(
        _mod_kernel,
        grid=(N // tn,),
        in_specs=[pl.BlockSpec((B, D), lambda j: (0, 0)),
                  pl.BlockSpec((D, tn), lambda j: (0, j)),
                  pl.BlockSpec((1, tn), lambda j: (0, j))],
        out_specs=pl.BlockSpec((B, tn), lambda j: (0, j)),
        out_shape=jax.ShapeDtypeStruct((B, N), F32),
        compiler_params=_params(1),
        name="mod",
    )(c, w_ada, b_ada.reshape(1, N))


def _in_proj_kernel(x_ref, sh_ref, sc_ref, wqT_ref, wk_ref, wvT_ref, wu_ref, wga_ref, wgp_ref,
                    qT_ref, k_ref, vT_ref, u_ref, ga_ref, gp_ref):
    h = _normalize(x_ref[0]) * (1.0 + sc_ref[0]) + sh_ref[0]
    hb = h.astype(BF16)
    k_ref[0] = _dot(hb, wk_ref[...]).astype(BF16)
    u_ref[0] = _dot(hb, wu_ref[...])
    ga_ref[0] = _dot(hb, wga_ref[...]).astype(BF16)
    gp_ref[0] = _dot(hb, wgp_ref[...]).astype(BF16)
    qT = _dot_nt(wqT_ref[...], hb).astype(BF16)
    for j in range(TOKEN_TILE // Q_TILE):
        qT_ref[0, j] = qT[:, j * Q_TILE:(j + 1) * Q_TILE]
    vT_ref[0] = _dot_nt(wvT_ref[...], hb).astype(BF16)


def _in_proj(x, sh1, sc1, wqT, wk, wvT, wu, wga, wgp):
    B, S, D = x.shape
    tm = TOKEN_TILE
    nq = tm // Q_TILE
    du = wu.shape[1]
    tok = lambda w: pl.BlockSpec((1, tm, w), lambda b, i: (b, i, 0))
    vec = pl.BlockSpec((1, 1, D), lambda b, i: (b, 0, 0))
    return pl.pallas_call(
        _in_proj_kernel,
        grid=(B, S // tm),
        in_specs=[tok(D), vec, vec,
                  _resident(wqT.shape), _resident(wk.shape), _resident(wvT.shape),
                  _resident(wu.shape), _resident(wga.shape), _resident(wgp.shape)],
        out_specs=[pl.BlockSpec((1, nq, D, Q_TILE), lambda b, i: (b, i, 0, 0)),
                   tok(D),
                   pl.BlockSpec((1, D, tm), lambda b, i: (b, 0, i)),
                   tok(du), tok(D), tok(D)],
        out_shape=[jax.ShapeDtypeStruct((B, S // Q_TILE, D, Q_TILE), BF16),
                   jax.ShapeDtypeStruct((B, S, D), BF16),
                   jax.ShapeDtypeStruct((B, D, S), BF16),
                   jax.ShapeDtypeStruct((B, S, du), F32),
                   jax.ShapeDtypeStruct((B, S, D), BF16),
                   jax.ShapeDtypeStruct((B, S, D), BF16)],
        compiler_params=_params(2),
        name="in_proj",
    )(x, sh1, sc1, wqT, wk, wvT, wu, wga, wgp)


def _attn_kernel(hc_ref, lq1_ref, lk1_ref, lq2_ref, lk2_ref, subg_ref, qT_ref, k_ref, vT_ref,
                 o_ref, kpos_ref, diag_ref, vext_ref, *slots, seq, lambda_init):
    b = pl.program_id(0)
    hp = pl.program_id(1)
    n_q = seq // Q_TILE
    assert KEY_CHUNK == Q_TILE
    z_refs, p_refs, m_refs = (slots[i * N_SLOTS:(i + 1) * N_SLOTS] for i in range(3))

    @pl.when((b == 0) & (hp == 0))
    def _():
        j = lax.broadcasted_iota(jnp.int32, kpos_ref.shape, 0)
        l = lax.broadcasted_iota(jnp.int32, kpos_ref.shape, 1)
        kpos = jnp.where(l < 2, 1, jnp.where(l == 2, j - j % Q_TILE,
                                             jnp.where(l == 3, j % Q_TILE, 0)))
        kpos_ref[...] = kpos.astype(F32).astype(BF16)
        r = lax.broadcasted_iota(jnp.int32, diag_ref.shape, 0)
        c = lax.broadcasted_iota(jnp.int32, diag_ref.shape, 1)
        diag_ref[...] = jnp.abs(r - c % Q_TILE).astype(F32)

    for hh in range(HEADS_PER_STEP):
        vext_ref[hh, 0:V_DIM, :] = vT_ref[0, hh * V_DIM:(hh + 1) * V_DIM, :]
        vext_ref[hh, V_DIM:, :] = jnp.ones((vext_ref.shape[1] - V_DIM, seq), BF16)

    lam = (jnp.exp(jnp.sum(lq1_ref[...] * lk1_ref[...], keepdims=True))
           - jnp.exp(jnp.sum(lq2_ref[...] * lk2_ref[...], keepdims=True))
           + lambda_init)
    out_gain = subg_ref[...] * (1.0 - lambda_init)

    heads = [hp * HEADS_PER_STEP + hh for hh in range(HEADS_PER_STEP)]
    q_scale = [hc_ref[0, h] for h in heads]
    c_exp = [hc_ref[1, h] for h in heads]

    row = lax.broadcasted_iota(jnp.int32, (2 * HEAD_DIM, Q_TILE), 0)
    pos_row = lax.broadcasted_iota(jnp.int32, (2 * HEAD_DIM, 2 * Q_TILE), 0)
    pos_col = lax.broadcasted_iota(jnp.int32, (2 * HEAD_DIM, 2 * Q_TILE), 1) % Q_TILE
    n_chunks = seq // KEY_CHUNK

    def scores_probs(si, pi):
        if si is not None:
            hh, qi = divmod(si, n_q)
            lanes = slice(hh * V_DIM, (hh + 1) * V_DIM)
            qs = qT_ref[0, qi, lanes, :].astype(F32) * q_scale[hh]
            qcat = jnp.concatenate([jnp.where(row < HEAD_DIM, qs, 0.0),
                                    jnp.where(row >= HEAD_DIM, qs, 0.0)], axis=1).astype(BF16)
            qpos = jnp.where(pos_row == 0, -qi * Q_TILE,
                             jnp.where(pos_row == 1, -pos_col,
                                       jnp.where(pos_row < 4, 1, 0))).astype(F32)
            q_before = jnp.concatenate([qcat, qpos.astype(BF16)], axis=0)
            q_after = jnp.concatenate([qcat, (-qpos).astype(BF16)], axis=0)
            z_ref = z_refs[si % N_SLOTS]
        if pi is not None:
            zp_ref, p_ref = z_refs[pi % N_SLOTS], p_refs[pi % N_SLOTS]
            mp = m_refs[pi % N_SLOTS][...]
            cp = c_exp[pi // n_q]
        m = None
        for c in range(n_chunks):
            rows = slice(c * KEY_CHUNK, (c + 1) * KEY_CHUNK)
            if si is not None:
                if c == qi:
                    z = _dot(k_ref[0, rows, lanes], qcat) - diag_ref[...]
                else:
                    kx = jnp.concatenate([k_ref[0, rows, lanes], kpos_ref[rows, :]], axis=1)
                    z = _dot(kx, q_before if c < qi else q_after)
                z_ref[rows, :] = z
                mc = jnp.max(z, axis=0, keepdims=True)
                m = mc if m is None else jnp.maximum(m, mc)
            if pi is not None:
                p_ref[rows, :] = jnp.exp2((zp_ref[rows, :] - mp) * cp).astype(BF16)
        if si is not None:
            m_refs[si % N_SLOTS][...] = m

    def values(vi):
        hh, qi = divmod(vi, n_q)
        acc = _dot(vext_ref[hh], p_refs[vi % N_SLOTS][...])
        o1 = acc[0:V_DIM, 0:Q_TILE] / acc[V_DIM:V_DIM + 1, 0:Q_TILE]
        o2 = acc[0:V_DIM, Q_TILE:] / acc[V_DIM:V_DIM + 1, Q_TILE:]
        a = o1 - lam * o2
        a = a * lax.rsqrt(jnp.mean(a * a, axis=0, keepdims=True) + RMS_EPS)
        o_ref[0, qi * Q_TILE:(qi + 1) * Q_TILE, hh * V_DIM:(hh + 1) * V_DIM] = (
            a.T * out_gain).astype(BF16)

    n_items = HEADS_PER_STEP * n_q
    for t in range(n_items + 2):
        scores_probs(t if t < n_items else None, t - 1 if 1 <= t <= n_items else None)
        if t >= 2:
            values(t - 2)


def _attn(qT, k, vT, lq1, lk1, lq2, lk2, sub_g, lambda_init):
    B, S, D = k.shape
    n_q = S // Q_TILE
    hw = HEADS_PER_STEP * V_DIM
    lam_spec = pl.BlockSpec((1, HEAD_DIM), lambda b, h: (0, 0))
    kern = functools.partial(_attn_kernel, seq=S, lambda_init=lambda_init)
    slopes = 2.0 ** -(np.arange(N_HEADS, dtype=np.float64) + 1.0)
    head_consts = jnp.asarray(np.stack([HEAD_DIM ** -0.5 / slopes, slopes * LOG2E]), F32)
    return pl.pallas_call(
        kern,
        grid=(B, N_HEADS // HEADS_PER_STEP),
        in_specs=[pl.BlockSpec(memory_space=pltpu.SMEM),
                  lam_spec, lam_spec, lam_spec, lam_spec,
                  pl.BlockSpec((1, V_DIM), lambda b, h: (0, 0)),
                  pl.BlockSpec((1, n_q, hw, Q_TILE), lambda b, h: (b, 0, h, 0)),
                  pl.BlockSpec((1, S, hw), lambda b, h: (b, 0, h)),
                  pl.BlockSpec((1, hw, S), lambda b, h: (b, h, 0))],
        out_specs=pl.BlockSpec((1, S, hw), lambda b, h: (b, 0, h)),
        out_shape=jax.ShapeDtypeStruct((B, S, D), BF16),
        scratch_shapes=[pltpu.VMEM((S, 2 * HEAD_DIM), BF16),
                        pltpu.VMEM((Q_TILE, 2 * Q_TILE), F32),
                        pltpu.VMEM((HEADS_PER_STEP, V_DIM + 16, S), BF16),
                        *[pltpu.VMEM((S, 2 * Q_TILE), F32)] * N_SLOTS,
                        *[pltpu.VMEM((S, 2 * Q_TILE), BF16)] * N_SLOTS,
                        *[pltpu.VMEM((1, 2 * Q_TILE), F32)] * N_SLOTS],
        compiler_params=_params(2),
        name="attn",
    )(head_consts, lq1, lk1, lq2, lk2, sub_g, qT, k, vT)


def _pooled_groups(u_ref, top_ref, bot_ref, seq):
    i = pl.program_id(1)
    tm = u_ref.shape[1]
    n = tm + 2 * POOL_PAD
    t = i * tm + lax.broadcasted_iota(jnp.int32, (tm, 1), 0)
    top = jnp.where(i > 0, top_ref[0], 0.0)
    bot = jnp.where(i < seq // tm - 1, bot_ref[0], 0.0)
    out = []
    for g, w in enumerate(POOL_WINDOWS):
        half = w // 2
        cols = slice(g * POOL_GROUP_IN, (g + 1) * POOL_GROUP_IN)
        ug = u_ref[0, :, cols]
        x = jnp.concatenate([top[:, cols], ug, bot[:, cols]], axis=0)
        win = x + pltpu.roll(x, 1, 0)
        step = 1
        while step < half:
            win = pltpu.roll(win, step, 0) + pltpu.roll(win, n - step, 0)
            step *= 2
        win = win[POOL_PAD:POOL_PAD + tm]
        cnt = (jnp.minimum(t + half, seq) - jnp.maximum(t - half, 0)).astype(F32)
        out.append((win / cnt - ug).astype(BF16))
    return out


def _mix_ffn_kernel(a_ref, u_ref, utop_ref, ubot_ref, ga_ref, gp_ref, x_ref,
                    g1_ref, sh2_ref, sc2_ref, g2_ref, wpool_ref, pscale_ref, wout_ref,
                    ln1g_ref, ln1b_ref, wg_ref, wu_ref, wo_ref, ln2g_ref, ln2b_ref, o_ref, *, seq):
    pooled = _pooled_groups(u_ref, utop_ref, ubot_ref, seq)
    tm = x_ref.shape[1]

    def gate_mix(rows):
        p_out = jnp.concatenate([_dot(pooled[g][rows], wpool_ref[g])
                                 for g in range(len(POOL_WINDOWS))], axis=-1) * pscale_ref[...]
        mixed = (jax.nn.sigmoid(ga_ref[0, rows, :].astype(F32)) * a_ref[0, rows, :].astype(F32)
                 + jax.nn.sigmoid(gp_ref[0, rows, :].astype(F32)) * p_out)
        return mixed.astype(BF16)

    def out_proj_norm(rows, mixed):
        y = _dot(mixed, wout_ref[...])
        x1 = (_normalize(ALPHA * x_ref[0, rows, :] + g1_ref[0] * y) * ln1g_ref[...]
              + ln1b_ref[...])
        h = (_normalize(x1) * (1.0 + sc2_ref[0]) + sh2_ref[0]).astype(BF16)
        return x1, h

    def swiglu_chunk(h, y, chunk):
        start, width = chunk
        gt = _dot(h, wg_ref[:, start:start + width])
        up = _dot(h, wu_ref[:, start:start + width])
        act = (gt * jax.nn.sigmoid(gt) * up).astype(BF16)
        part = _dot(act, wo_ref[start:start + width, :])
        return part if y is None else y + part

    def finish(rows, x1, y):
        o_ref[0, rows, :] = (_normalize(ALPHA * x1 + g2_ref[0] * y) * ln2g_ref[...]
                             + ln2b_ref[...])

    rows_a, rows_b = slice(0, tm // 2), slice(tm // 2, tm)
    x1_a, h_a = out_proj_norm(rows_a, gate_mix(rows_a))
    y_a = swiglu_chunk(h_a, None, FF_CHUNKS[0])
    mixed_b = gate_mix(rows_b)
    y_a = swiglu_chunk(h_a, y_a, FF_CHUNKS[1])
    x1_b, h_b = out_proj_norm(rows_b, mixed_b)
    for chunk in FF_CHUNKS[2:]:
        y_a = swiglu_chunk(h_a, y_a, chunk)
    y_b = swiglu_chunk(h_b, None, FF_CHUNKS[0])
    finish(rows_a, x1_a, y_a)
    for chunk in FF_CHUNKS[1:]:
        y_b = swiglu_chunk(h_b, y_b, chunk)
    finish(rows_b, x1_b, y_b)


def _mix_ffn(a, u, ga, gp, x, g1, sh2, sc2, g2, *weights):
    B, S, D = x.shape
    tm = TOKEN_TILE
    pad_blocks = tm // POOL_PAD
    tok = lambda w: pl.BlockSpec((1, tm, w), lambda b, i: (b, i, 0))
    vec = pl.BlockSpec((1, 1, D), lambda b, i: (b, 0, 0))
    du = u.shape[-1]
    top = pl.BlockSpec((1, POOL_PAD, du), lambda b, i: (b, jnp.maximum(i * pad_blocks - 1, 0), 0))
    bot = pl.BlockSpec((1, POOL_PAD, du),
                       lambda b, i: (b, jnp.minimum((i + 1) * pad_blocks, S // POOL_PAD - 1), 0))
    return pl.pallas_call(
        functools.partial(_mix_ffn_kernel, seq=S),
        grid=(B, S // tm),
        in_specs=[tok(D), tok(du), top, bot, tok(D), tok(D), tok(D), vec, vec, vec, vec,
                  *[_resident(w.shape) for w in weights]],
        out_specs=tok(D),
        out_shape=jax.ShapeDtypeStruct((B, S, D), F32),
        compiler_params=_params(2),
        name="mix_ffn",
    )(a, u, u, u, ga, gp, x, g1, sh2, sc2, g2, *weights)


def kernel(x, c, w_ada, b_ada, w_in, lambda_q1, lambda_k1, lambda_q2, lambda_k2, sub_g, w_pool,
           pool_scale, w_out, ln1_g, ln1_b, w_ffn_in, w_ffn_out, ln2_g, ln2_b):
    B, S, D = x.shape
    assert w_ada.shape[0] == DEPTH == 1
    assert S % TOKEN_TILE == 0 and TOKEN_TILE % Q_TILE == 0
    d_ff = w_ffn_out.shape[1]
    assert sum(w for _, w in FF_CHUNKS) == d_ff
    pool_in = len(POOL_WINDOWS) * POOL_GROUP_IN
    qk = N_HEADS * 2 * HEAD_DIM
    lambda_init = 0.8 - 0.6 * math.exp(-0.3 * 0)

    mod = _mod(c, w_ada[0], b_ada[0])
    sh1, sc1, g1, sh2, sc2, g2 = [m.reshape(B, 1, D) for m in jnp.split(mod, N_MOD, axis=-1)]

    w = w_in[0].astype(BF16)
    bounds = np.cumsum([qk, qk, D, pool_in, D]).tolist()
    wq, wk, wv, wu, wga, wgp = jnp.split(w, bounds, axis=-1)
    qT, k, vT, u, ga, gp = _in_proj(x, sh1, sc1, wq.T, wk, wv.T, wu, wga, wgp)

    row = lambda v: v.reshape(1, -1)
    a = _attn(qT, k, vT, row(lambda_q1[0]), row(lambda_k1[0]), row(lambda_q2[0]),
              row(lambda_k2[0]), row(sub_g[0]), lambda_init)
    wf = w_ffn_in[0].astype(BF16)
    return _mix_ffn(a, u, ga, gp, x, g1, sh2, sc2, g2,
                    w_pool[0].astype(BF16), row(pool_scale[0]), w_out[0].astype(BF16),
                    row(ln1_g[0]), row(ln1_b[0]),
                    wf[:, :d_ff], wf[:, d_ff:], w_ffn_out[0].astype(BF16),
                    row(ln2_g[0]), row(ln2_b[0]))
```

```python
import functools
import math

import jax
import jax.numpy as jnp
import numpy as np
from jax import lax
from jax.experimental import pallas as pl
from jax.experimental.pallas import tpu as pltpu

F32 = jnp.float32
BF16 = jnp.bfloat16

N_HEADS = 8
HEAD_DIM = 64
V_DIM = 2 * HEAD_DIM
POOL_WINDOWS = (2, 4, 8, 16)
POOL_GROUP_IN = 128
N_MOD = 6
DEPTH = 1
ALPHA = (2 * DEPTH) ** 0.25
LN_EPS = 1e-5
RMS_EPS = 1e-5
LOG2E = 1.4426950408889634

V7X_VMEM_LIMIT_BYTES = 56 * 1024 * 1024

TOKEN_TILE = 512
Q_TILE = 256
KEY_CHUNK = 256
N_SLOTS = 3
HEADS_PER_STEP = 1
POOL_PAD = 8
FF_CHUNKS = ((0, 1024), (1024, 1024), (2048, 768))


def _params(n_grid_dims):
    return pltpu.CompilerParams(
        dimension_semantics=("arbitrary",) * n_grid_dims,
        vmem_limit_bytes=V7X_VMEM_LIMIT_BYTES)


def _resident(shape):
    nd = len(shape)
    return pl.BlockSpec(shape, lambda *_: (0,) * nd, pipeline_mode=pl.Buffered(1))


def _dot(a, b):
    return jnp.dot(a, b, preferred_element_type=F32)


def _dot_nt(a, b):
    return lax.dot_general(a, b, (((1,), (1,)), ((), ())), preferred_element_type=F32)


def _normalize(x):
    mu = jnp.mean(x, axis=-1, keepdims=True)
    xc = x - mu
    var = jnp.mean(xc * xc, axis=-1, keepdims=True)
    return xc * lax.rsqrt(var + LN_EPS)


def _mod_kernel(c_ref, w_ref, b_ref, o_ref):
    c = c_ref[...]
    ca = c * jax.nn.sigmoid(c)
    ca_hi = ca.astype(BF16)
    ca_lo = (ca - ca_hi.astype(F32)).astype(BF16)
    w = w_ref[...]
    w_hi = w.astype(BF16)
    w_lo = (w - w_hi.astype(F32)).astype(BF16)
    acc = _dot(ca_hi, w_hi) + _dot(ca_lo, w_hi) + _dot(ca_hi, w_lo)
    o_ref[...] = acc + b_ref[...]


def _mod(c, w_ada, b_ada):
    B, D = c.shape
    N = w_ada.shape[1]
    tn = 1536
    return pl.pallas_call(
        _mod_kernel,
        grid=(N // tn,),
        in_specs=[pl.BlockSpec((B, D), lambda j: (0, 0)),
                  pl.BlockSpec((D, tn), lambda j: (0, j)),
                  pl.BlockSpec((1, tn), lambda j: (0, j))],
        out_specs=pl.BlockSpec((B, tn), lambda j: (0, j)),
        out_shape=jax.ShapeDtypeStruct((B, N), F32),
        compiler_params=_params(1),
        name="mod",
    )(c, w_ada, b_ada.reshape(1, N))


def _in_proj_kernel(x_ref, sh_ref, sc_ref, wqT_ref, wk_ref, wvT_ref, wu_ref, wga_ref, wgp_ref,
                    qT_ref, k_ref, vT_ref, u_ref, ga_ref, gp_ref):
    h = _normalize(x_ref[0]) * (1.0 + sc_ref[0]) + sh_ref[0]
    hb = h.astype(BF16)
    k_ref[0] = _dot(hb, wk_ref[...]).astype(BF16)
    u_ref[0] = _dot(hb, wu_ref[...])
    ga_ref[0] = _dot(hb, wga_ref[...]).astype(BF16)
    gp_ref[0] = _dot(hb, wgp_ref[...]).astype(BF16)
    qT = _dot_nt(wqT_ref[...], hb).astype(BF16)
    for j in range(TOKEN_TILE // Q_TILE):
        qT_ref[0, j] = qT[:, j * Q_TILE:(j + 1) * Q_TILE]
    vT_ref[0] = _dot_nt(wvT_ref[...], hb).astype(BF16)


def _in_proj(x, sh1, sc1, wqT, wk, wvT, wu, wga, wgp):
    B, S, D = x.shape
    tm = TOKEN_TILE
    nq = tm // Q_TILE
    du = wu.shape[1]
    tok = lambda w: pl.BlockSpec((1, tm, w), lambda b, i: (b, i, 0))
    vec = pl.BlockSpec((1, 1, D), lambda b, i: (b, 0, 0))
    return pl.pallas_call(
        _in_proj_kernel,
        grid=(B, S // tm),
        in_specs=[tok(D), vec, vec,
                  _resident(wqT.shape), _resident(wk.shape), _resident(wvT.shape),
                  _resident(wu.shape), _resident(wga.shape), _resident(wgp.shape)],
        out_specs=[pl.BlockSpec((1, nq, D, Q_TILE), lambda b, i: (b, i, 0, 0)),
                   tok(D),
                   pl.BlockSpec((1, D, tm), lambda b, i: (b, 0, i)),
                   tok(du), tok(D), tok(D)],
        out_shape=[jax.ShapeDtypeStruct((B, S // Q_TILE, D, Q_TILE), BF16),
                   jax.ShapeDtypeStruct((B, S, D), BF16),
                   jax.ShapeDtypeStruct((B, D, S), BF16),
                   jax.ShapeDtypeStruct((B, S, du), F32),
                   jax.ShapeDtypeStruct((B, S, D), BF16),
                   jax.ShapeDtypeStruct((B, S, D), BF16)],
        compiler_params=_params(2),
        name="in_proj",
    )(x, sh1, sc1, wqT, wk, wvT, wu, wga, wgp)


def _attn_kernel(hc_ref, lq1_ref, lk1_ref, lq2_ref, lk2_ref, subg_ref, qT_ref, k_ref, vT_ref,
                 o_ref, kpos_ref, diag_ref, vext_ref, *slots, seq, lambda_init):
    b = pl.program_id(0)
    hp = pl.program_id(1)
    n_q = seq // Q_TILE
    assert KEY_CHUNK == Q_TILE
    z_refs, p_refs, m_refs = (slots[i * N_SLOTS:(i + 1) * N_SLOTS] for i in range(3))

    @pl.when((b == 0) & (hp == 0))
    def _():
        j = lax.broadcasted_iota(jnp.int32, kpos_ref.shape, 0)
        l = lax.broadcasted_iota(jnp.int32, kpos_ref.shape, 1)
        kpos = jnp.where(l < 2, 1, jnp.where(l == 2, j - j % Q_TILE,
                                             jnp.where(l == 3, j % Q_TILE, 0)))
        kpos_ref[...] = kpos.astype(F32).astype(BF16)
        r = lax.broadcasted_iota(jnp.int32, diag_ref.shape, 0)
        c = lax.broadcasted_iota(jnp.int32, diag_ref.shape, 1)
        diag_ref[...] = jnp.abs(r - c % Q_TILE).astype(F32)

    for hh in range(HEADS_PER_STEP):
        vext_ref[hh, 0:V_DIM, :] = vT_ref[0, hh * V_DIM:(hh + 1) * V_DIM, :]
        vext_ref[hh, V_DIM:, :] = jnp.ones((vext_ref.shape[1] - V_DIM, seq), BF16)

    lam = (jnp.exp(jnp.sum(lq1_ref[...] * lk1_ref[...], keepdims=True))
           - jnp.exp(jnp.sum(lq2_ref[...] * lk2_ref[...], keepdims=True))
           + lambda_init)
    out_gain = subg_ref[...] * (1.0 - lambda_init)

    heads = [hp * HEADS_PER_STEP + hh for hh in range(HEADS_PER_STEP)]
    q_scale = [hc_ref[0, h] for h in heads]
    c_exp = [hc_ref[1, h] for h in heads]

    row = lax.broadcasted_iota(jnp.int32, (2 * HEAD_DIM, Q_TILE), 0)
    pos_row = lax.broadcasted_iota(jnp.int32, (2 * HEAD_DIM, 2 * Q_TILE), 0)
    pos_col = lax.broadcasted_iota(jnp.int32, (2 * HEAD_DIM, 2 * Q_TILE), 1) % Q_TILE
    n_chunks = seq // KEY_CHUNK

    def scores_probs(si, pi):
        if si is not None:
            hh, qi = divmod(si, n_q)
            lanes = slice(hh * V_DIM, (hh + 1) * V_DIM)
            qs = qT_ref[0, qi, lanes, :].astype(F32) * q_scale[hh]
            qcat = jnp.concatenate([jnp.where(row < HEAD_DIM, qs, 0.0),
                                    jnp.where(row >= HEAD_DIM, qs, 0.0)], axis=1).astype(BF16)
            qpos = jnp.where(pos_row == 0, -qi * Q_TILE,
                             jnp.where(pos_row == 1, -pos_col,
                                       jnp.where(pos_row < 4, 1, 0))).astype(F32)
            q_before = jnp.concatenate([qcat, qpos.astype(BF16)], axis=0)
            q_after = jnp.concatenate([qcat, (-qpos).astype(BF16)], axis=0)
            z_ref = z_refs[si % N_SLOTS]
        if pi is not None:
            zp_ref, p_ref = z_refs[pi % N_SLOTS], p_refs[pi % N_SLOTS]
            mp = m_refs[pi % N_SLOTS][...]
            cp = c_exp[pi // n_q]
        m = None
        for c in range(n_chunks):
            rows = slice(c * KEY_CHUNK, (c + 1) * KEY_CHUNK)
            if si is not None:
                if c == qi:
                    z = _dot(k_ref[0, rows, lanes], qcat) - diag_ref[...]
                else:
                    kx = jnp.concatenate([k_ref[0, rows, lanes], kpos_ref[rows, :]], axis=1)
                    z = _dot(kx, q_before if c < qi else q_after)
                z_ref[rows, :] = z
                mc = jnp.max(z, axis=0, keepdims=True)
                m = mc if m is None else jnp.maximum(m, mc)
            if pi is not None:
                p_ref[rows, :] = jnp.exp2((zp_ref[rows, :] - mp) * cp).astype(BF16)
        if si is not None:
            m_refs[si % N_SLOTS][...] = m

    def values(vi):
        hh, qi = divmod(vi, n_q)
        acc = _dot(vext_ref[hh], p_refs[vi % N_SLOTS][...])
        o1 = acc[0:V_DIM, 0:Q_TILE] / acc[V_DIM:V_DIM + 1, 0:Q_TILE]
        o2 = acc[0:V_DIM, Q_TILE:] / acc[V_DIM:V_DIM + 1, Q_TILE:]
        a = o1 - lam * o2
        a = a * lax.rsqrt(jnp.mean(a * a, axis=0, keepdims=True) + RMS_EPS)
        o_ref[0, qi * Q_TILE:(qi + 1) * Q_TILE, hh * V_DIM:(hh + 1) * V_DIM] = (
            a.T * out_gain).astype(BF16)

    n_items = HEADS_PER_STEP * n_q
    for t in range(n_items + 2):
        scores_probs(t if t < n_items else None, t - 1 if 1 <= t <= n_items else None)
        if t >= 2:
            values(t - 2)


def _attn(qT, k, vT, lq1, lk1, lq2, lk2, sub_g, lambda_init):
    B, S, D = k.shape
    n_q = S // Q_TILE
    hw = HEADS_PER_STEP * V_DIM
    lam_spec = pl.BlockSpec((1, HEAD_DIM), lambda b, h: (0, 0))
    kern = functools.partial(_attn_kernel, seq=S, lambda_init=lambda_init)
    slopes = 2.0 ** -(np.arange(N_HEADS, dtype=np.float64) + 1.0)
    head_consts = jnp.asarray(np.stack([HEAD_DIM ** -0.5 / slopes, slopes * LOG2E]), F32)
    return pl.pallas_call(
        kern,
        grid=(B, N_HEADS // HEADS_PER_STEP),
        in_specs=[pl.BlockSpec(memory_space=pltpu.SMEM),
                  lam_spec, lam_spec, lam_spec, lam_spec,
                  pl.BlockSpec((1, V_DIM), lambda b, h: (0, 0)),
                  pl.BlockSpec((1, n_q, hw, Q_TILE), lambda b, h: (b, 0, h, 0)),
                  pl.BlockSpec((1, S, hw), lambda b, h: (b, 0, h)),
                  pl.BlockSpec((1, hw, S), lambda b, h: (b, h, 0))],
        out_specs=pl.BlockSpec((1, S, hw), lambda b, h: (b, 0, h)),
        out_shape=jax.ShapeDtypeStruct((B, S, D), BF16),
        scratch_shapes=[pltpu.VMEM((S, 2 * HEAD_DIM), BF16),
                        pltpu.VMEM((Q_TILE, 2 * Q_TILE), F32),
                        pltpu.VMEM((HEADS_PER_STEP, V_DIM + 16, S), BF16),
                        *[pltpu.VMEM((S, 2 * Q_TILE), F32)] * N_SLOTS,
                        *[pltpu.VMEM((S, 2 * Q_TILE), BF16)] * N_SLOTS,
                        *[pltpu.VMEM((1, 2 * Q_TILE), F32)] * N_SLOTS],
        compiler_params=_params(2),
        name="attn",
    )(head_consts, lq1, lk1, lq2, lk2, sub_g, qT, k, vT)


def _pooled_groups(u_ref, top_ref, bot_ref, i, seq):
    tm = u_ref.shape[1]
    n = tm + 2 * POOL_PAD
    t = i * tm + lax.broadcasted_iota(jnp.int32, (tm, 1), 0)
    top = jnp.where(i > 0, top_ref[0], 0.0)
    bot = jnp.where(i < seq // tm - 1, bot_ref[0], 0.0)
    out = []
    for g, w in enumerate(POOL_WINDOWS):
        half = w // 2
        cols = slice(g * POOL_GROUP_IN, (g + 1) * POOL_GROUP_IN)
        ug = u_ref[0, :, cols]
        x = jnp.concatenate([top[:, cols], ug, bot[:, cols]], axis=0)
        win = x + pltpu.roll(x, 1, 0)
        step = 1
        while step < half:
            win = pltpu.roll(win, step, 0) + pltpu.roll(win, n - step, 0)
            step *= 2
        win = win[POOL_PAD:POOL_PAD + tm]
        cnt = (jnp.minimum(t + half, seq) - jnp.maximum(t - half, 0)).astype(F32)
        out.append((win / cnt - ug).astype(BF16))
    return out


def _mix_ffn_kernel(a_ref, u_ref, utop_ref, ubot_ref, ga_ref, gp_ref, x_ref,
                    g1_ref, sh2_ref, sc2_ref, g2_ref, wpool_ref, pscale_ref, wout_ref,
                    ln1g_ref, ln1b_ref, wg_ref, wu_ref, wo_ref, ln2g_ref, ln2b_ref,
                    o_ref, x1_carry, h_carry, *, seq, n_tiles):
    s = pl.program_id(0)
    tm = x_ref.shape[1]
    halves = (slice(0, tm // 2), slice(tm // 2, tm))
    slot = s % 2

    def pooled_groups():
        i = jnp.minimum(s, n_tiles - 1) % (seq // tm)
        return _pooled_groups(u_ref, utop_ref, ubot_ref, i, seq)

    def gate_mix(pooled, rows):
        p_out = jnp.concatenate([_dot(pooled[g][rows], wpool_ref[g])
                                 for g in range(len(POOL_WINDOWS))], axis=-1) * pscale_ref[...]
        mixed = (jax.nn.sigmoid(ga_ref[0, rows, :].astype(F32)) * a_ref[0, rows, :].astype(F32)
                 + jax.nn.sigmoid(gp_ref[0, rows, :].astype(F32)) * p_out)
        return mixed.astype(BF16)

    def out_proj_norm(rows, mixed):
        y = _dot(mixed, wout_ref[...])
        x1 = (_normalize(ALPHA * x_ref[0, rows, :] + g1_ref[0] * y) * ln1g_ref[...]
              + ln1b_ref[...])
        x1_carry[slot, rows, :] = x1
        h_carry[slot, rows, :] = (_normalize(x1) * (1.0 + sc2_ref[0]) + sh2_ref[0]).astype(BF16)

    def swiglu_chunk(h, y, chunk):
        start, width = chunk
        gt = _dot(h, wg_ref[:, start:start + width])
        up = _dot(h, wu_ref[:, start:start + width])
        act = (gt * jax.nn.sigmoid(gt) * up).astype(BF16)
        part = _dot(act, wo_ref[start:start + width, :])
        return part if y is None else y + part

    def finish(rows, y):
        x1 = x1_carry[1 - slot, rows, :]
        o_ref[0, rows, :] = (_normalize(ALPHA * x1 + g2_ref[0] * y) * ln2g_ref[...]
                             + ln2b_ref[...])

    @pl.when(s == 0)
    def _():
        pooled = pooled_groups()
        for rows in halves:
            out_proj_norm(rows, gate_mix(pooled, rows))

    @pl.when((s > 0) & (s < n_tiles))
    def _():
        pooled = pooled_groups()
        for rows in halves:
            h = h_carry[1 - slot, rows, :]
            y = swiglu_chunk(h, None, FF_CHUNKS[0])
            mixed = gate_mix(pooled, rows)
            y = swiglu_chunk(h, y, FF_CHUNKS[1])
            out_proj_norm(rows, mixed)
            for chunk in FF_CHUNKS[2:]:
                y = swiglu_chunk(h, y, chunk)
            finish(rows, y)

    @pl.when(s == n_tiles)
    def _():
        for rows in halves:
            h = h_carry[1 - slot, rows, :]
            y = None
            for chunk in FF_CHUNKS:
                y = swiglu_chunk(h, y, chunk)
            finish(rows, y)


def _mix_ffn(a, u, ga, gp, x, g1, sh2, sc2, g2, *weights):
    B, S, D = x.shape
    tm = TOKEN_TILE
    nt = S // tm
    n_tiles = B * nt
    pad_blocks = tm // POOL_PAD
    du = u.shape[-1]

    def front(s):
        t = jnp.minimum(s, n_tiles - 1)
        return t // nt, t % nt

    def back(s):
        t = jnp.maximum(s - 1, 0)
        return t // nt, t % nt

    tok = lambda w: pl.BlockSpec((1, tm, w), lambda s: (*front(s), 0))
    vec = pl.BlockSpec((1, 1, D), lambda s: (front(s)[0], 0, 0))
    top = pl.BlockSpec((1, POOL_PAD, du), lambda s: (
        front(s)[0], jnp.maximum(front(s)[1] * pad_blocks - 1, 0), 0))
    bot = pl.BlockSpec((1, POOL_PAD, du), lambda s: (
        front(s)[0], jnp.minimum((front(s)[1] + 1) * pad_blocks, S // POOL_PAD - 1), 0))
    return pl.pallas_call(
        functools.partial(_mix_ffn_kernel, seq=S, n_tiles=n_tiles),
        grid=(n_tiles + 1,),
        in_specs=[tok(D), tok(du), top, bot, tok(D), tok(D), tok(D), vec, vec, vec,
                  pl.BlockSpec((1, 1, D), lambda s: (back(s)[0], 0, 0)),
                  *[_resident(w.shape) for w in weights]],
        out_specs=pl.BlockSpec((1, tm, D), lambda s: (*back(s), 0)),
        out_shape=jax.ShapeDtypeStruct((B, S, D), F32),
        scratch_shapes=[pltpu.VMEM((2, tm, D), F32), pltpu.VMEM((2, tm, D), BF16)],
        compiler_params=_params(1),
        name="mix_ffn",
    )(a, u, u, u, ga, gp, x, g1, sh2, sc2, g2, *weights)


def kernel(x, c, w_ada, b_ada, w_in, lambda_q1, lambda_k1, lambda_q2, lambda_k2, sub_g, w_pool,
           pool_scale, w_out, ln1_g, ln1_b, w_ffn_in, w_ffn_out, ln2_g, ln2_b):
    B, S, D = x.shape
    assert w_ada.shape[0] == DEPTH == 1
    assert S % TOKEN_TILE == 0 and TOKEN_TILE % Q_TILE == 0
    d_ff = w_ffn_out.shape[1]
    assert sum(w for _, w in FF_CHUNKS) == d_ff
    pool_in = len(POOL_WINDOWS) * POOL_GROUP_IN
    qk = N_HEADS * 2 * HEAD_DIM
    lambda_init = 0.8 - 0.6 * math.exp(-0.3 * 0)

    mod = _mod(c, w_ada[0], b_ada[0])
    sh1, sc1, g1, sh2, sc2, g2 = [m.reshape(B, 1, D) for m in jnp.split(mod, N_MOD, axis=-1)]

    w = w_in[0].astype(BF16)
    bounds = np.cumsum([qk, qk, D, pool_in, D]).tolist()
    wq, wk, wv, wu, wga, wgp = jnp.split(w, bounds, axis=-1)
    qT, k, vT, u, ga, gp = _in_proj(x, sh1, sc1, wq.T, wk, wv.T, wu, wga, wgp)

    row = lambda v: v.reshape(1, -1)
    a = _attn(qT, k, vT, row(lambda_q1[0]), row(lambda_k1[0]), row(lambda_q2[0]),
              row(lambda_k2[0]), row(sub_g[0]), lambda_init)
    wf = w_ffn_in[0].astype(BF16)
    return _mix_ffn(a, u, ga, gp, x, g1, sh2, sc2, g2,
                    w_pool[0].astype(BF16), row(pool_scale[0]), w_out[0].astype(BF16),
                    row(ln1_g[0]), row(ln1_b[0]),
                    wf[:, :d_ff], wf[:, d_ff:], w_ffn_out[0].astype(BF16),
                    row(ln2_g[0]), row(ln2_b[0]))
```

```python
import functools
import math

import jax
import jax.numpy as jnp
import numpy as np
from jax import lax
from jax.experimental import pallas as pl
from jax.experimental.pallas import tpu as pltpu

F32 = jnp.float32
BF16 = jnp.bfloat16

N_HEADS = 8
HEAD_DIM = 64
V_DIM = 2 * HEAD_DIM
POOL_WINDOWS = (2, 4, 8, 16)
POOL_GROUP_IN = 128
N_MOD = 6
DEPTH = 1
ALPHA = (2 * DEPTH) ** 0.25
LN_EPS = 1e-5
RMS_EPS = 1e-5
LOG2E = 1.4426950408889634

V7X_VMEM_LIMIT_BYTES = 56 * 1024 * 1024

TOKEN_TILE = 512
Q_TILE = 256
KEY_CHUNK = 256
N_SLOTS = 3
POS_ROWS = 16
POOL_PAD = 8
FF_CHUNKS = ((0, 1024), (1024, 1024), (2048, 768))


def _params(n_grid_dims):
    return pltpu.CompilerParams(
        dimension_semantics=("arbitrary",) * n_grid_dims,
        vmem_limit_bytes=V7X_VMEM_LIMIT_BYTES)


def _resident(shape):
    nd = len(shape)
    return pl.BlockSpec(shape, lambda *_: (0,) * nd, pipeline_mode=pl.Buffered(1))


def _dot(a, b):
    return jnp.dot(a, b, preferred_element_type=F32)


def _dot_nt(a, b):
    return lax.dot_general(a, b, (((1,), (1,)), ((), ())), preferred_element_type=F32)


def _normalize(x):
    mu = jnp.mean(x, axis=-1, keepdims=True)
    xc = x - mu
    var = jnp.mean(xc * xc, axis=-1, keepdims=True)
    return xc * lax.rsqrt(var + LN_EPS)


def _mod_kernel(c_ref, w_ref, b_ref, o_ref):
    c = c_ref[...]
    ca = c * jax.nn.sigmoid(c)
    ca_hi = ca.astype(BF16)
    ca_lo = (ca - ca_hi.astype(F32)).astype(BF16)
    w = w_ref[...]
    w_hi = w.astype(BF16)
    w_lo = (w - w_hi.astype(F32)).astype(BF16)
    acc = _dot(ca_hi, w_hi) + _dot(ca_lo, w_hi) + _dot(ca_hi, w_lo)
    o_ref[...] = acc + b_ref[...]


def _mod(c, w_ada, b_ada):
    B, D = c.shape
    N = w_ada.shape[1]
    tn = 1536
    return pl.pallas_call(
        _mod_kernel,
        grid=(N // tn,),
        in_specs=[pl.BlockSpec((B, D), lambda j: (0, 0)),
                  pl.BlockSpec((D, tn), lambda j: (0, j)),
                  pl.BlockSpec((1, tn), lambda j: (0, j))],
        out_specs=pl.BlockSpec((B, tn), lambda j: (0, j)),
        out_shape=jax.ShapeDtypeStruct((B, N), F32),
        compiler_params=_params(1),
        name="mod",
    )(c, w_ada, b_ada.reshape(1, N))


def _in_proj_kernel(x_ref, sh_ref, sc_ref, wqT_ref, wk_ref, wvT_ref, wu_ref, wga_ref, wgp_ref,
                    qT_ref, k_ref, vT_ref, u_ref, ga_ref, gp_ref):
    h = _normalize(x_ref[0]) * (1.0 + sc_ref[0]) + sh_ref[0]
    hb = h.astype(BF16)
    k_ref[0] = _dot(hb, wk_ref[...]).astype(BF16)
    u_ref[0] = _dot(hb, wu_ref[...])
    ga_ref[0] = _dot(hb, wga_ref[...]).astype(BF16)
    gp_ref[0] = _dot(hb, wgp_ref[...]).astype(BF16)
    qT = _dot_nt(wqT_ref[...], hb).astype(BF16)
    for j in range(TOKEN_TILE // Q_TILE):
        qT_ref[0, j] = qT[:, j * Q_TILE:(j + 1) * Q_TILE]
    vT_ref[0] = _dot_nt(wvT_ref[...], hb).astype(BF16)


def _in_proj(x, sh1, sc1, wqT, wk, wvT, wu, wga, wgp):
    B, S, D = x.shape
    tm = TOKEN_TILE
    nq = tm // Q_TILE
    du = wu.shape[1]
    tok = lambda w: pl.BlockSpec((1, tm, w), lambda b, i: (b, i, 0))
    vec = pl.BlockSpec((1, 1, D), lambda b, i: (b, 0, 0))
    return pl.pallas_call(
        _in_proj_kernel,
        grid=(B, S // tm),
        in_specs=[tok(D), vec, vec,
                  _resident(wqT.shape), _resident(wk.shape), _resident(wvT.shape),
                  _resident(wu.shape), _resident(wga.shape), _resident(wgp.shape)],
        out_specs=[pl.BlockSpec((1, nq, D, Q_TILE), lambda b, i: (b, i, 0, 0)),
                   tok(D),
                   pl.BlockSpec((1, D, tm), lambda b, i: (b, 0, i)),
                   tok(du), tok(D), tok(D)],
        out_shape=[jax.ShapeDtypeStruct((B, S // Q_TILE, D, Q_TILE), BF16),
                   jax.ShapeDtypeStruct((B, S, D), BF16),
                   jax.ShapeDtypeStruct((B, D, S), BF16),
                   jax.ShapeDtypeStruct((B, S, du), F32),
                   jax.ShapeDtypeStruct((B, S, D), BF16),
                   jax.ShapeDtypeStruct((B, S, D), BF16)],
        compiler_params=_params(2),
        name="in_proj",
    )(x, sh1, sc1, wqT, wk, wvT, wu, wga, wgp)


def _bf16_parts(x):
    x1 = x.astype(BF16).astype(F32)
    x2 = (x - x1).astype(BF16).astype(F32)
    return x1, x2, x - x1 - x2


def _attn_kernel(hc_ref, lq1_ref, lk1_ref, lq2_ref, lk2_ref, subg_ref, qT_ref, k_ref, vT_ref,
                 o_ref, kpos_ref, diag_ref, vext_ref, *slots, seq, lambda_init):
    b = pl.program_id(0)
    h = pl.program_id(1)
    n_q = seq // Q_TILE
    assert KEY_CHUNK == Q_TILE
    z_refs, p_refs, m_refs = (slots[i * N_SLOTS:(i + 1) * N_SLOTS] for i in range(3))

    @pl.when((b == 0) & (h == 0))
    def _():
        j = lax.broadcasted_iota(jnp.int32, kpos_ref.shape, 0)
        l = lax.broadcasted_iota(jnp.int32, kpos_ref.shape, 1)
        kpos = jnp.where(l < 3, 1, jnp.where(l < 6, j - j % Q_TILE,
                                             jnp.where(l < 9, j % Q_TILE, 0)))
        kpos_ref[...] = kpos.astype(F32).astype(BF16)
        r = lax.broadcasted_iota(jnp.int32, diag_ref.shape, 0)
        c = lax.broadcasted_iota(jnp.int32, diag_ref.shape, 1)
        diag_ref[...] = jnp.abs(r - c % Q_TILE).astype(F32)

    vext_ref[0:V_DIM, :] = vT_ref[0]
    vext_ref[V_DIM:, :] = jnp.ones((vext_ref.shape[0] - V_DIM, seq), BF16)

    lam = (jnp.exp(jnp.sum(lq1_ref[...] * lk1_ref[...], keepdims=True))
           - jnp.exp(jnp.sum(lq2_ref[...] * lk2_ref[...], keepdims=True))
           + lambda_init)
    out_gain = subg_ref[...] * (1.0 - lambda_init)

    c_bias = hc_ref[0, h]
    row = lax.broadcasted_iota(jnp.int32, (2 * HEAD_DIM, Q_TILE), 0)
    pos_row = lax.broadcasted_iota(jnp.int32, (POS_ROWS, 2 * Q_TILE), 0)
    pos_col = lax.broadcasted_iota(jnp.int32, (POS_ROWS, 2 * Q_TILE), 1) % Q_TILE
    c_parts = _bf16_parts(jnp.full((POS_ROWS, 2 * Q_TILE), c_bias, F32))
    pos_pad = jnp.zeros((2 * HEAD_DIM - POS_ROWS, 2 * Q_TILE), BF16)
    n_chunks = seq // KEY_CHUNK

    def scores_probs(qi, pi):
        if qi is not None:
            qs = qT_ref[0, qi].astype(F32) * (LOG2E * HEAD_DIM ** -0.5)
            qcat = jnp.concatenate([jnp.where(row < HEAD_DIM, qs, 0.0),
                                    jnp.where(row >= HEAD_DIM, qs, 0.0)], axis=1).astype(BF16)
            ci_parts = _bf16_parts(-c_bias * (qi * Q_TILE + pos_col).astype(F32))
            qpos = jnp.zeros((POS_ROWS, 2 * Q_TILE), F32)
            for r, part in enumerate(ci_parts + c_parts + c_parts):
                qpos = jnp.where(pos_row == r, part, qpos)
            q_before = jnp.concatenate([qcat, qpos.astype(BF16), pos_pad], axis=0)
            q_after = jnp.concatenate([qcat, (-qpos).astype(BF16), pos_pad], axis=0)
            z_ref = z_refs[qi % N_SLOTS]
        if pi is not None:
            zp_ref, p_ref = z_refs[pi % N_SLOTS], p_refs[pi % N_SLOTS]
            mp = m_refs[pi % N_SLOTS][...]
        m = None
        for c in range(n_chunks):
            rows = slice(c * KEY_CHUNK, (c + 1) * KEY_CHUNK)
            if qi is not None:
                if c == qi:
                    z = _dot(k_ref[0, rows, :], qcat) - c_bias * diag_ref[...]
                else:
                    kx = jnp.concatenate([k_ref[0, rows, :], kpos_ref[rows, :]], axis=1)
                    z = _dot(kx, q_before if c < qi else q_after)
                z_ref[rows, :] = z
                mc = jnp.max(z, axis=0, keepdims=True)
                m = mc if m is None else jnp.maximum(m, mc)
            if pi is not None:
                p_ref[rows, :] = jnp.exp2(zp_ref[rows, :] - mp).astype(BF16)
        if qi is not None:
            m_refs[qi % N_SLOTS][...] = m

    def values(qi):
        acc = _dot(vext_ref[...], p_refs[qi % N_SLOTS][...])
        o1 = acc[0:V_DIM, 0:Q_TILE] / acc[V_DIM:V_DIM + 1, 0:Q_TILE]
        o2 = acc[0:V_DIM, Q_TILE:] / acc[V_DIM:V_DIM + 1, Q_TILE:]
        a = o1 - lam * o2
        a = a * lax.rsqrt(jnp.mean(a * a, axis=0, keepdims=True) + RMS_EPS)
        o_ref[0, qi * Q_TILE:(qi + 1) * Q_TILE, :] = (a.T * out_gain).astype(BF16)

    for t in range(n_q + 2):
        scores_probs(t if t < n_q else None, t - 1 if 1 <= t <= n_q else None)
        if t >= 2:
            values(t - 2)


def _attn(qT, k, vT, lq1, lk1, lq2, lk2, sub_g, lambda_init):
    B, S, D = k.shape
    n_q = S // Q_TILE
    lam_spec = pl.BlockSpec((1, HEAD_DIM), lambda b, h: (0, 0))
    kern = functools.partial(_attn_kernel, seq=S, lambda_init=lambda_init)
    slopes = 2.0 ** -(np.arange(N_HEADS, dtype=np.float64) + 1.0)
    head_consts = jnp.asarray((slopes * LOG2E)[None, :], F32)
    return pl.pallas_call(
        kern,
        grid=(B, N_HEADS),
        in_specs=[pl.BlockSpec(memory_space=pltpu.SMEM),
                  lam_spec, lam_spec, lam_spec, lam_spec,
                  pl.BlockSpec((1, V_DIM), lambda b, h: (0, 0)),
                  pl.BlockSpec((1, n_q, V_DIM, Q_TILE), lambda b, h: (b, 0, h, 0)),
                  pl.BlockSpec((1, S, V_DIM), lambda b, h: (b, 0, h)),
                  pl.BlockSpec((1, V_DIM, S), lambda b, h: (b, h, 0))],
        out_specs=pl.BlockSpec((1, S, V_DIM), lambda b, h: (b, 0, h)),
        out_shape=jax.ShapeDtypeStruct((B, S, D), BF16),
        scratch_shapes=[pltpu.VMEM((S, 2 * HEAD_DIM), BF16),
                        pltpu.VMEM((Q_TILE, 2 * Q_TILE), F32),
                        pltpu.VMEM((V_DIM + 16, S), BF16),
                        *[pltpu.VMEM((S, 2 * Q_TILE), F32)] * N_SLOTS,
                        *[pltpu.VMEM((S, 2 * Q_TILE), BF16)] * N_SLOTS,
                        *[pltpu.VMEM((1, 2 * Q_TILE), F32)] * N_SLOTS],
        compiler_params=_params(2),
        name="attn",
    )(head_consts, lq1, lk1, lq2, lk2, sub_g, qT, k, vT)


def _pooled_groups(u_ref, top_ref, bot_ref, i, seq):
    tm = u_ref.shape[1]
    n = tm + 2 * POOL_PAD
    t = i * tm + lax.broadcasted_iota(jnp.int32, (tm, 1), 0)
    top = jnp.where(i > 0, top_ref[0], 0.0)
    bot = jnp.where(i < seq // tm - 1, bot_ref[0], 0.0)
    out = []
    for g, w in enumerate(POOL_WINDOWS):
        half = w // 2
        cols = slice(g * POOL_GROUP_IN, (g + 1) * POOL_GROUP_IN)
        ug = u_ref[0, :, cols]
        x = jnp.concatenate([top[:, cols], ug, bot[:, cols]], axis=0)
        win = x + pltpu.roll(x, 1, 0)
        step = 1
        while step < half:
            win = pltpu.roll(win, step, 0) + pltpu.roll(win, n - step, 0)
            step *= 2
        win = win[POOL_PAD:POOL_PAD + tm]
        cnt = (jnp.minimum(t + half, seq) - jnp.maximum(t - half, 0)).astype(F32)
        out.append((win / cnt - ug).astype(BF16))
    return out


def _mix_ffn_kernel(a_ref, u_ref, utop_ref, ubot_ref, ga_ref, gp_ref, x_ref,
                    g1_ref, sh2_ref, sc2_ref, g2_ref, wpool_ref, pscale_ref, wout_ref,
                    ln1g_ref, ln1b_ref, wg_ref, wu_ref, wo_ref, ln2g_ref, ln2b_ref,
                    o_ref, x1_carry, h_carry, *, seq, n_tiles):
    s = pl.program_id(0)
    tm = x_ref.shape[1]
    halves = (slice(0, tm // 2), slice(tm // 2, tm))
    slot = s % 2

    def pooled_groups():
        i = jnp.minimum(s, n_tiles - 1) % (seq // tm)
        return _pooled_groups(u_ref, utop_ref, ubot_ref, i, seq)

    def gate_mix(pooled, rows):
        p_out = jnp.concatenate([_dot(pooled[g][rows], wpool_ref[g])
                                 for g in range(len(POOL_WINDOWS))], axis=-1) * pscale_ref[...]
        mixed = (jax.nn.sigmoid(ga_ref[0, rows, :].astype(F32)) * a_ref[0, rows, :].astype(F32)
                 + jax.nn.sigmoid(gp_ref[0, rows, :].astype(F32)) * p_out)
        return mixed.astype(BF16)

    def out_proj_norm(rows, mixed):
        y = _dot(mixed, wout_ref[...])
        x1 = (_normalize(ALPHA * x_ref[0, rows, :] + g1_ref[0] * y) * ln1g_ref[...]
              + ln1b_ref[...])
        x1_carry[slot, rows, :] = x1
        h_carry[slot, rows, :] = (_normalize(x1) * (1.0 + sc2_ref[0]) + sh2_ref[0]).astype(BF16)

    def swiglu_chunk(h, y, chunk):
        start, width = chunk
        gt = _dot(h, wg_ref[:, start:start + width])
        up = _dot(h, wu_ref[:, start:start + width])
        act = (gt * jax.nn.sigmoid(gt) * up).astype(BF16)
        part = _dot(act, wo_ref[start:start + width, :])
        return part if y is None else y + part

    def finish(rows, y):
        x1 = x1_carry[1 - slot, rows, :]
        o_ref[0, rows, :] = (_normalize(ALPHA * x1 + g2_ref[0] * y) * ln2g_ref[...]
                             + ln2b_ref[...])

    @pl.when(s == 0)
    def _():
        pooled = pooled_groups()
        for rows in halves:
            out_proj_norm(rows, gate_mix(pooled, rows))

    @pl.when((s > 0) & (s < n_tiles))
    def _():
        pooled = pooled_groups()
        for rows in halves:
            h = h_carry[1 - slot, rows, :]
            y = swiglu_chunk(h, None, FF_CHUNKS[0])
            mixed = gate_mix(pooled, rows)
            y = swiglu_chunk(h, y, FF_CHUNKS[1])
            out_proj_norm(rows, mixed)
            for chunk in FF_CHUNKS[2:]:
                y = swiglu_chunk(h, y, chunk)
            finish(rows, y)

    @pl.when(s == n_tiles)
    def _():
        for rows in halves:
            h = h_carry[1 - slot, rows, :]
            y = None
            for chunk in FF_CHUNKS:
                y = swiglu_chunk(h, y, chunk)
            finish(rows, y)


def _mix_ffn(a, u, ga, gp, x, g1, sh2, sc2, g2, *weights):
    B, S, D = x.shape
    tm = TOKEN_TILE
    nt = S // tm
    n_tiles = B * nt
    pad_blocks = tm // POOL_PAD
    du = u.shape[-1]

    def front(s):
        t = jnp.minimum(s, n_tiles - 1)
        return t // nt, t % nt

    def back(s):
        t = jnp.maximum(s - 1, 0)
        return t // nt, t % nt

    tok = lambda w: pl.BlockSpec((1, tm, w), lambda s: (*front(s), 0))
    vec = pl.BlockSpec((1, 1, D), lambda s: (front(s)[0], 0, 0))
    top = pl.BlockSpec((1, POOL_PAD, du), lambda s: (
        front(s)[0], jnp.maximum(front(s)[1] * pad_blocks - 1, 0), 0))
    bot = pl.BlockSpec((1, POOL_PAD, du), lambda s: (
        front(s)[0], jnp.minimum((front(s)[1] + 1) * pad_blocks, S // POOL_PAD - 1), 0))
    return pl.pallas_call(
        functools.partial(_mix_ffn_kernel, seq=S, n_tiles=n_tiles),
        grid=(n_tiles + 1,),
        in_specs=[tok(D), tok(du), top, bot, tok(D), tok(D), tok(D), vec, vec, vec,
                  pl.BlockSpec((1, 1, D), lambda s: (back(s)[0], 0, 0)),
                  *[_resident(w.shape) for w in weights]],
        out_specs=pl.BlockSpec((1, tm, D), lambda s: (*back(s), 0)),
        out_shape=jax.ShapeDtypeStruct((B, S, D), F32),
        scratch_shapes=[pltpu.VMEM((2, tm, D), F32), pltpu.VMEM((2, tm, D), BF16)],
        compiler_params=_params(1),
        name="mix_ffn",
    )(a, u, u, u, ga, gp, x, g1, sh2, sc2, g2, *weights)


def kernel(x, c, w_ada, b_ada, w_in, lambda_q1, lambda_k1, lambda_q2, lambda_k2, sub_g, w_pool,
           pool_scale, w_out, ln1_g, ln1_b, w_ffn_in, w_ffn_out, ln2_g, ln2_b):
    B, S, D = x.shape
    assert w_ada.shape[0] == DEPTH == 1
    assert S % TOKEN_TILE == 0 and TOKEN_TILE % Q_TILE == 0
    d_ff = w_ffn_out.shape[1]
    assert sum(w for _, w in FF_CHUNKS) == d_ff
    pool_in = len(POOL_WINDOWS) * POOL_GROUP_IN
    qk = N_HEADS * 2 * HEAD_DIM
    lambda_init = 0.8 - 0.6 * math.exp(-0.3 * 0)

    mod = _mod(c, w_ada[0], b_ada[0])
    sh1, sc1, g1, sh2, sc2, g2 = [m.reshape(B, 1, D) for m in jnp.split(mod, N_MOD, axis=-1)]

    w = w_in[0].astype(BF16)
    bounds = np.cumsum([qk, qk, D, pool_in, D]).tolist()
    wq, wk, wv, wu, wga, wgp = jnp.split(w, bounds, axis=-1)
    qT, k, vT, u, ga, gp = _in_proj(x, sh1, sc1, wq.T, wk, wv.T, wu, wga, wgp)

    row = lambda v: v.reshape(1, -1)
    a = _attn(qT, k, vT, row(lambda_q1[0]), row(lambda_k1[0]), row(lambda_q2[0]),
              row(lambda_k2[0]), row(sub_g[0]), lambda_init)
    wf = w_ffn_in[0].astype(BF16)
    return _mix_ffn(a, u, ga, gp, x, g1, sh2, sc2, g2,
                    w_pool[0].astype(BF16), row(pool_scale[0]), w_out[0].astype(BF16),
                    row(ln1_g[0]), row(ln1_b[0]),
                    wf[:, :d_ff], wf[:, d_ff:], w_ffn_out[0].astype(BF16),
                    row(ln2_g[0]), row(ln2_b[0]))
```

```python
import functools
import math

import jax
import jax.numpy as jnp
import numpy as np
from jax import lax
from jax.experimental import pallas as pl
from jax.experimental.pallas import tpu as pltpu

F32 = jnp.float32
BF16 = jnp.bfloat16

N_HEADS = 8
HEAD_DIM = 64
V_DIM = 2 * HEAD_DIM
POOL_WINDOWS = (2, 4, 8, 16)
POOL_GROUP_IN = 128
N_MOD = 6
DEPTH = 1
ALPHA = (2 * DEPTH) ** 0.25
LN_EPS = 1e-5
RMS_EPS = 1e-5
LOG2E = 1.4426950408889634

V7X_VMEM_LIMIT_BYTES = 56 * 1024 * 1024

TOKEN_TILE = 512
Q_TILE = 256
KEY_CHUNK = 256
N_SLOTS = 2
POS_ROWS = 16
POOL_PAD = 8
FF_CHUNKS = ((0, 1024), (1024, 1024), (2048, 768))


def _params(n_grid_dims):
    return pltpu.CompilerParams(
        dimension_semantics=("arbitrary",) * n_grid_dims,
        vmem_limit_bytes=V7X_VMEM_LIMIT_BYTES)


def _resident(shape):
    nd = len(shape)
    return pl.BlockSpec(shape, lambda *_: (0,) * nd, pipeline_mode=pl.Buffered(1))


def _dot(a, b):
    return jnp.dot(a, b, preferred_element_type=F32)


def _dot_nt(a, b):
    return lax.dot_general(a, b, (((1,), (1,)), ((), ())), preferred_element_type=F32)


def _normalize(x):
    mu = jnp.mean(x, axis=-1, keepdims=True)
    xc = x - mu
    var = jnp.mean(xc * xc, axis=-1, keepdims=True)
    return xc * lax.rsqrt(var + LN_EPS)


def _mod_kernel(c_ref, w_ref, b_ref, o_ref):
    c = c_ref[...]
    ca = c * jax.nn.sigmoid(c)
    ca_hi = ca.astype(BF16)
    ca_lo = (ca - ca_hi.astype(F32)).astype(BF16)
    w = w_ref[...]
    w_hi = w.astype(BF16)
    w_lo = (w - w_hi.astype(F32)).astype(BF16)
    acc = _dot(ca_hi, w_hi) + _dot(ca_lo, w_hi) + _dot(ca_hi, w_lo)
    o_ref[...] = acc + b_ref[...]


def _mod(c, w_ada, b_ada):
    B, D = c.shape
    N = w_ada.shape[1]
    tn = 1536
    return pl.pallas_call(
        _mod_kernel,
        grid=(N // tn,),
        in_specs=[pl.BlockSpec((B, D), lambda j: (0, 0)),
                  pl.BlockSpec((D, tn), lambda j: (0, j)),
                  pl.BlockSpec((1, tn), lambda j: (0, j))],
        out_specs=pl.BlockSpec((B, tn), lambda j: (0, j)),
        out_shape=jax.ShapeDtypeStruct((B, N), F32),
        compiler_params=_params(1),
        name="mod",
    )(c, w_ada, b_ada.reshape(1, N))


def _in_proj_kernel(x_ref, sh_ref, sc_ref, wqT_ref, wk_ref, wvT_ref, wu_ref, wga_ref, wgp_ref,
                    qT_ref, k_ref, vT_ref, u_ref, ga_ref, gp_ref):
    h = _normalize(x_ref[0]) * (1.0 + sc_ref[0]) + sh_ref[0]
    hb = h.astype(BF16)
    k_ref[0] = _dot(hb, wk_ref[...]).astype(BF16)
    u_ref[0] = _dot(hb, wu_ref[...])
    ga_ref[0] = _dot(hb, wga_ref[...]).astype(BF16)
    gp_ref[0] = _dot(hb, wgp_ref[...]).astype(BF16)
    qT = _dot_nt(wqT_ref[...], hb).astype(BF16)
    for j in range(TOKEN_TILE // Q_TILE):
        qT_ref[0, j] = qT[:, j * Q_TILE:(j + 1) * Q_TILE]
    vT_ref[0] = _dot_nt(wvT_ref[...], hb).astype(BF16)


def _in_proj(x, sh1, sc1, wqT, wk, wvT, wu, wga, wgp):
    B, S, D = x.shape
    tm = TOKEN_TILE
    nq = tm // Q_TILE
    du = wu.shape[1]
    tok = lambda w: pl.BlockSpec((1, tm, w), lambda b, i: (b, i, 0))
    vec = pl.BlockSpec((1, 1, D), lambda b, i: (b, 0, 0))
    return pl.pallas_call(
        _in_proj_kernel,
        grid=(B, S // tm),
        in_specs=[tok(D), vec, vec,
                  _resident(wqT.shape), _resident(wk.shape), _resident(wvT.shape),
                  _resident(wu.shape), _resident(wga.shape), _resident(wgp.shape)],
        out_specs=[pl.BlockSpec((1, nq, D, Q_TILE), lambda b, i: (b, i, 0, 0)),
                   tok(D),
                   pl.BlockSpec((1, D, tm), lambda b, i: (b, 0, i)),
                   tok(du), tok(D), tok(D)],
        out_shape=[jax.ShapeDtypeStruct((B, S // Q_TILE, D, Q_TILE), BF16),
                   jax.ShapeDtypeStruct((B, S, D), BF16),
                   jax.ShapeDtypeStruct((B, D, S), BF16),
                   jax.ShapeDtypeStruct((B, S, du), F32),
                   jax.ShapeDtypeStruct((B, S, D), BF16),
                   jax.ShapeDtypeStruct((B, S, D), BF16)],
        compiler_params=_params(2),
        name="in_proj",
    )(x, sh1, sc1, wqT, wk, wvT, wu, wga, wgp)


def _bf16_parts(x):
    x1 = x.astype(BF16).astype(F32)
    x2 = (x - x1).astype(BF16).astype(F32)
    return x1, x2, x - x1 - x2


def _attn_kernel(hc_ref, lq1_ref, lk1_ref, lq2_ref, lk2_ref, subg_ref, qT_ref, k_ref, vT_ref,
                 o_ref, kpos_ref, diag_ref, vext_ref, *slots, seq, lambda_init):
    b = pl.program_id(0)
    h = pl.program_id(1)
    n_q = seq // Q_TILE
    assert KEY_CHUNK == Q_TILE
    z_refs, p_refs, m_refs = (slots[i * N_SLOTS:(i + 1) * N_SLOTS] for i in range(3))

    @pl.when((b == 0) & (h == 0))
    def _():
        j = lax.broadcasted_iota(jnp.int32, kpos_ref.shape, 0)
        l = lax.broadcasted_iota(jnp.int32, kpos_ref.shape, 1)
        kpos = jnp.where(l < 3, 1, jnp.where(l < 6, j - j % Q_TILE,
                                             jnp.where(l < 9, j % Q_TILE, 0)))
        kpos_ref[...] = kpos.astype(F32).astype(BF16)
        r = lax.broadcasted_iota(jnp.int32, diag_ref.shape, 0)
        c = lax.broadcasted_iota(jnp.int32, diag_ref.shape, 1)
        diag_ref[...] = jnp.abs(r - c % Q_TILE).astype(F32)

    vext_ref[0:V_DIM, :] = vT_ref[0]
    vext_ref[V_DIM:, :] = jnp.ones((vext_ref.shape[0] - V_DIM, seq), BF16)

    lam = (jnp.exp(jnp.sum(lq1_ref[...] * lk1_ref[...], keepdims=True))
           - jnp.exp(jnp.sum(lq2_ref[...] * lk2_ref[...], keepdims=True))
           + lambda_init)
    out_gain = subg_ref[...] * (1.0 - lambda_init)

    c_bias = hc_ref[0, h]
    row = lax.broadcasted_iota(jnp.int32, (2 * HEAD_DIM, Q_TILE), 0)
    pos_row = lax.broadcasted_iota(jnp.int32, (POS_ROWS, 2 * Q_TILE), 0)
    pos_col = lax.broadcasted_iota(jnp.int32, (POS_ROWS, 2 * Q_TILE), 1) % Q_TILE
    c_parts = _bf16_parts(jnp.full((POS_ROWS, 2 * Q_TILE), c_bias, F32))
    pos_pad = jnp.zeros((2 * HEAD_DIM - POS_ROWS, 2 * Q_TILE), BF16)
    n_chunks = seq // KEY_CHUNK

    def scores_probs(qi, pi):
        if qi is not None:
            qs = qT_ref[0, qi].astype(F32) * (LOG2E * HEAD_DIM ** -0.5)
            qcat = jnp.concatenate([jnp.where(row < HEAD_DIM, qs, 0.0),
                                    jnp.where(row >= HEAD_DIM, qs, 0.0)], axis=1).astype(BF16)
            ci_parts = _bf16_parts(-c_bias * (qi * Q_TILE + pos_col).astype(F32))
            qpos = jnp.zeros((POS_ROWS, 2 * Q_TILE), F32)
            for r, part in enumerate(ci_parts + c_parts + c_parts):
                qpos = jnp.where(pos_row == r, part, qpos)
            q_before = jnp.concatenate([qcat, qpos.astype(BF16), pos_pad], axis=0)
            q_after = jnp.concatenate([qcat, (-qpos).astype(BF16), pos_pad], axis=0)
            z_ref = z_refs[qi % N_SLOTS]
        if pi is not None:
            zp_ref, p_ref = z_refs[pi % N_SLOTS], p_refs[pi % N_SLOTS]
            mp = m_refs[pi % N_SLOTS][...]
        m = None
        for c in range(n_chunks):
            rows = slice(c * KEY_CHUNK, (c + 1) * KEY_CHUNK)
            if qi is not None:
                if c == qi:
                    z = _dot(k_ref[0, rows, :], qcat) - c_bias * diag_ref[...]
                else:
                    kx = jnp.concatenate([k_ref[0, rows, :], kpos_ref[rows, :]], axis=1)
                    z = _dot(kx, q_before if c < qi else q_after)
                z_ref[rows, :] = z
                mc = jnp.max(z, axis=0, keepdims=True)
                m = mc if m is None else jnp.maximum(m, mc)
            if pi is not None:
                p_ref[rows, :] = jnp.exp2(zp_ref[rows, :] - mp).astype(BF16)
        if qi is not None:
            m_refs[qi % N_SLOTS][...] = m

    def values(qi):
        acc = _dot(vext_ref[...], p_refs[qi % N_SLOTS][...])
        o1 = acc[0:V_DIM, 0:Q_TILE] / acc[V_DIM:V_DIM + 1, 0:Q_TILE]
        o2 = acc[0:V_DIM, Q_TILE:] / acc[V_DIM:V_DIM + 1, Q_TILE:]
        a = o1 - lam * o2
        a = a * lax.rsqrt(jnp.mean(a * a, axis=0, keepdims=True) + RMS_EPS)
        o_ref[0, qi * Q_TILE:(qi + 1) * Q_TILE, :] = (a.T * out_gain).astype(BF16)

    for t in range(n_q + 1):
        scores_probs(t if t < n_q else None, t - 1 if t >= 1 else None)
        if t >= 1:
            values(t - 1)


def _attn(qT, k, vT, lq1, lk1, lq2, lk2, sub_g, lambda_init):
    B, S, D = k.shape
    n_q = S // Q_TILE
    lam_spec = pl.BlockSpec((1, HEAD_DIM), lambda b, h: (0, 0))
    kern = functools.partial(_attn_kernel, seq=S, lambda_init=lambda_init)
    slopes = 2.0 ** -(np.arange(N_HEADS, dtype=np.float64) + 1.0)
    head_consts = jnp.asarray((slopes * LOG2E)[None, :], F32)
    return pl.pallas_call(
        kern,
        grid=(B, N_HEADS),
        in_specs=[pl.BlockSpec(memory_space=pltpu.SMEM),
                  lam_spec, lam_spec, lam_spec, lam_spec,
                  pl.BlockSpec((1, V_DIM), lambda b, h: (0, 0)),
                  pl.BlockSpec((1, n_q, V_DIM, Q_TILE), lambda b, h: (b, 0, h, 0)),
                  pl.BlockSpec((1, S, V_DIM), lambda b, h: (b, 0, h)),
                  pl.BlockSpec((1, V_DIM, S), lambda b, h: (b, h, 0))],
        out_specs=pl.BlockSpec((1, S, V_DIM), lambda b, h: (b, 0, h)),
        out_shape=jax.ShapeDtypeStruct((B, S, D), BF16),
        scratch_shapes=[pltpu.VMEM((S, 2 * HEAD_DIM), BF16),
                        pltpu.VMEM((Q_TILE, 2 * Q_TILE), F32),
                        pltpu.VMEM((V_DIM + 16, S), BF16),
                        *[pltpu.VMEM((S, 2 * Q_TILE), F32)] * N_SLOTS,
                        *[pltpu.VMEM((S, 2 * Q_TILE), BF16)] * N_SLOTS,
                        *[pltpu.VMEM((1, 2 * Q_TILE), F32)] * N_SLOTS],
        compiler_params=_params(2),
        name="attn",
    )(head_consts, lq1, lk1, lq2, lk2, sub_g, qT, k, vT)


def _pooled_groups(u_ref, top_ref, bot_ref, i, seq):
    tm = u_ref.shape[1]
    n = tm + 2 * POOL_PAD
    t = i * tm + lax.broadcasted_iota(jnp.int32, (tm, 1), 0)
    top = jnp.where(i > 0, top_ref[0], 0.0)
    bot = jnp.where(i < seq // tm - 1, bot_ref[0], 0.0)
    out = []
    for g, w in enumerate(POOL_WINDOWS):
        half = w // 2
        cols = slice(g * POOL_GROUP_IN, (g + 1) * POOL_GROUP_IN)
        ug = u_ref[0, :, cols]
        x = jnp.concatenate([top[:, cols], ug, bot[:, cols]], axis=0)
        win = x + pltpu.roll(x, 1, 0)
        step = 1
        while step < half:
            win = pltpu.roll(win, step, 0) + pltpu.roll(win, n - step, 0)
            step *= 2
        win = win[POOL_PAD:POOL_PAD + tm]
        cnt = (jnp.minimum(t + half, seq) - jnp.maximum(t - half, 0)).astype(F32)
        out.append((win / cnt - ug).astype(BF16))
    return out


def _mix_ffn_kernel(a_ref, u_ref, utop_ref, ubot_ref, ga_ref, gp_ref, x_ref,
                    g1_ref, sh2_ref, sc2_ref, g2_ref, wpool_ref, pscale_ref, wout_ref,
                    ln1g_ref, ln1b_ref, wg_ref, wu_ref, wo_ref, ln2g_ref, ln2b_ref,
                    o_ref, x1_carry, h_carry, *, seq, n_tiles):
    s = pl.program_id(0)
    tm = x_ref.shape[1]
    halves = (slice(0, tm // 2), slice(tm // 2, tm))
    slot = s % 2

    def pooled_groups():
        i = jnp.minimum(s, n_tiles - 1) % (seq // tm)
        return _pooled_groups(u_ref, utop_ref, ubot_ref, i, seq)

    def gate_mix(pooled, rows):
        p_out = jnp.concatenate([_dot(pooled[g][rows], wpool_ref[g])
                                 for g in range(len(POOL_WINDOWS))], axis=-1) * pscale_ref[...]
        mixed = (jax.nn.sigmoid(ga_ref[0, rows, :].astype(F32)) * a_ref[0, rows, :].astype(F32)
                 + jax.nn.sigmoid(gp_ref[0, rows, :].astype(F32)) * p_out)
        return mixed.astype(BF16)

    def out_proj_norm(rows, mixed):
        y = _dot(mixed, wout_ref[...])
        x1 = (_normalize(ALPHA * x_ref[0, rows, :] + g1_ref[0] * y) * ln1g_ref[...]
              + ln1b_ref[...])
        x1_carry[slot, rows, :] = x1
        h_carry[slot, rows, :] = (_normalize(x1) * (1.0 + sc2_ref[0]) + sh2_ref[0]).astype(BF16)

    def swiglu_chunk(h, y, chunk):
        start, width = chunk
        gt = _dot(h, wg_ref[:, start:start + width])
        up = _dot(h, wu_ref[:, start:start + width])
        act = (gt * jax.nn.sigmoid(gt) * up).astype(BF16)
        part = _dot(act, wo_ref[start:start + width, :])
        return part if y is None else y + part

    def finish(rows, y):
        x1 = x1_carry[1 - slot, rows, :]
        o_ref[0, rows, :] = (_normalize(ALPHA * x1 + g2_ref[0] * y) * ln2g_ref[...]
                             + ln2b_ref[...])

    @pl.when(s == 0)
    def _():
        pooled = pooled_groups()
        for rows in halves:
            out_proj_norm(rows, gate_mix(pooled, rows))

    @pl.when((s > 0) & (s < n_tiles))
    def _():
        pooled = pooled_groups()
        for rows in halves:
            h = h_carry[1 - slot, rows, :]
            y = swiglu_chunk(h, None, FF_CHUNKS[0])
            mixed = gate_mix(pooled, rows)
            y = swiglu_chunk(h, y, FF_CHUNKS[1])
            out_proj_norm(rows, mixed)
            for chunk in FF_CHUNKS[2:]:
                y = swiglu_chunk(h, y, chunk)
            finish(rows, y)

    @pl.when(s == n_tiles)
    def _():
        for rows in halves:
            h = h_carry[1 - slot, rows, :]
            y = None
            for chunk in FF_CHUNKS:
                y = swiglu_chunk(h, y, chunk)
            finish(rows, y)


def _mix_ffn(a, u, ga, gp, x, g1, sh2, sc2, g2, *weights):
    B, S, D = x.shape
    tm = TOKEN_TILE
    nt = S // tm
    n_tiles = B * nt
    pad_blocks = tm // POOL_PAD
    du = u.shape[-1]

    def front(s):
        t = jnp.minimum(s, n_tiles - 1)
        return t // nt, t % nt

    def back(s):
        t = jnp.maximum(s - 1, 0)
        return t // nt, t % nt

    tok = lambda w: pl.BlockSpec((1, tm, w), lambda s: (*front(s), 0))
    vec = pl.BlockSpec((1, 1, D), lambda s: (front(s)[0], 0, 0))
    top = pl.BlockSpec((1, POOL_PAD, du), lambda s: (
        front(s)[0], jnp.maximum(front(s)[1] * pad_blocks - 1, 0), 0))
    bot = pl.BlockSpec((1, POOL_PAD, du), lambda s: (
        front(s)[0], jnp.minimum((front(s)[1] + 1) * pad_blocks, S // POOL_PAD - 1), 0))
    return pl.pallas_call(
        functools.partial(_mix_ffn_kernel, seq=S, n_tiles=n_tiles),
        grid=(n_tiles + 1,),
        in_specs=[tok(D), tok(du), top, bot, tok(D), tok(D), tok(D), vec, vec, vec,
                  pl.BlockSpec((1, 1, D), lambda s: (back(s)[0], 0, 0)),
                  *[_resident(w.shape) for w in weights]],
        out_specs=pl.BlockSpec((1, tm, D), lambda s: (*back(s), 0)),
        out_shape=jax.ShapeDtypeStruct((B, S, D), F32),
        scratch_shapes=[pltpu.VMEM((2, tm, D), F32), pltpu.VMEM((2, tm, D), BF16)],
        compiler_params=_params(1),
        name="mix_ffn",
    )(a, u, u, u, ga, gp, x, g1, sh2, sc2, g2, *weights)


def kernel(x, c, w_ada, b_ada, w_in, lambda_q1, lambda_k1, lambda_q2, lambda_k2, sub_g, w_pool,
           pool_scale, w_out, ln1_g, ln1_b, w_ffn_in, w_ffn_out, ln2_g, ln2_b):
    B, S, D = x.shape
    assert w_ada.shape[0] == DEPTH == 1
    assert S % TOKEN_TILE == 0 and TOKEN_TILE % Q_TILE == 0
    d_ff = w_ffn_out.shape[1]
    assert sum(w for _, w in FF_CHUNKS) == d_ff
    pool_in = len(POOL_WINDOWS) * POOL_GROUP_IN
    qk = N_HEADS * 2 * HEAD_DIM
    lambda_init = 0.8 - 0.6 * math.exp(-0.3 * 0)

    mod = _mod(c, w_ada[0], b_ada[0])
    sh1, sc1, g1, sh2, sc2, g2 = [m.reshape(B, 1, D) for m in jnp.split(mod, N_MOD, axis=-1)]

    w = w_in[0].astype(BF16)
    bounds = np.cumsum([qk, qk, D, pool_in, D]).tolist()
    wq, wk, wv, wu, wga, wgp = jnp.split(w, bounds, axis=-1)
    qT, k, vT, u, ga, gp = _in_proj(x, sh1, sc1, wq.T, wk, wv.T, wu, wga, wgp)

    row = lambda v: v.reshape(1, -1)
    a = _attn(qT, k, vT, row(lambda_q1[0]), row(lambda_k1[0]), row(lambda_q2[0]),
              row(lambda_k2[0]), row(sub_g[0]), lambda_init)
    wf = w_ffn_in[0].astype(BF16)
    return _mix_ffn(a, u, ga, gp, x, g1, sh2, sc2, g2,
                    w_pool[0].astype(BF16), row(pool_scale[0]), w_out[0].astype(BF16),
                    row(ln1_g[0]), row(ln1_b[0]),
                    wf[:, :d_ff], wf[:, d_ff:], w_ffn_out[0].astype(BF16),
                    row(ln2_g[0]), row(ln2_b[0]))
```

```python
import functools
import math

import jax
import jax.numpy as jnp
import numpy as np
from jax import lax
from jax.experimental import pallas as pl
from jax.experimental.pallas import tpu as pltpu

F32 = jnp.float32
BF16 = jnp.bfloat16

N_HEADS = 8
HEAD_DIM = 64
V_DIM = 2 * HEAD_DIM
POOL_WINDOWS = (2, 4, 8, 16)
POOL_GROUP_IN = 128
N_MOD = 6
DEPTH = 1
ALPHA = (2 * DEPTH) ** 0.25
LN_EPS = 1e-5
RMS_EPS = 1e-5
LOG2E = 1.4426950408889634

V7X_VMEM_LIMIT_BYTES = 56 * 1024 * 1024

TOKEN_TILE = 512
Q_TILE = 256
KEY_CHUNK = 256
N_SLOTS = 2
POS_ROWS = 16
POOL_PAD = 8
FF_CHUNKS = ((0, 1024), (1024, 1024), (2048, 768))


def _params(n_grid_dims):
    return pltpu.CompilerParams(
        dimension_semantics=("arbitrary",) * n_grid_dims,
        vmem_limit_bytes=V7X_VMEM_LIMIT_BYTES)


def _resident(shape):
    nd = len(shape)
    return pl.BlockSpec(shape, lambda *_: (0,) * nd, pipeline_mode=pl.Buffered(1))


def _dot(a, b):
    return jnp.dot(a, b, preferred_element_type=F32)


def _dot_nt(a, b):
    return lax.dot_general(a, b, (((1,), (1,)), ((), ())), preferred_element_type=F32)


def _normalize(x):
    mu = jnp.mean(x, axis=-1, keepdims=True)
    xc = x - mu
    var = jnp.mean(xc * xc, axis=-1, keepdims=True)
    return xc * lax.rsqrt(var + LN_EPS)


def _mod_kernel(c_ref, w_ref, b_ref, o_ref):
    c = c_ref[...]
    ca = c * jax.nn.sigmoid(c)
    ca_hi = ca.astype(BF16)
    ca_lo = (ca - ca_hi.astype(F32)).astype(BF16)
    w = w_ref[...]
    w_hi = w.astype(BF16)
    w_lo = (w - w_hi.astype(F32)).astype(BF16)
    acc = _dot(ca_hi, w_hi) + _dot(ca_lo, w_hi) + _dot(ca_hi, w_lo)
    o_ref[...] = acc + b_ref[...]


def _mod(c, w_ada, b_ada):
    B, D = c.shape
    N = w_ada.shape[1]
    tn = 1536
    return pl.pallas_call(
        _mod_kernel,
        grid=(N // tn,),
        in_specs=[pl.BlockSpec((B, D), lambda j: (0, 0)),
                  pl.BlockSpec((D, tn), lambda j: (0, j)),
                  pl.BlockSpec((1, tn), lambda j: (0, j))],
        out_specs=pl.BlockSpec((B, tn), lambda j: (0, j)),
        out_shape=jax.ShapeDtypeStruct((B, N), F32),
        compiler_params=_params(1),
        name="mod",
    )(c, w_ada, b_ada.reshape(1, N))


def _in_proj_kernel(x_ref, sh_ref, sc_ref, wqT_ref, wk_ref, wvT_ref, wu_ref, wga_ref, wgp_ref,
                    qT_ref, k_ref, vT_ref, u_ref, ga_ref, gp_ref):
    h = _normalize(x_ref[0]) * (1.0 + sc_ref[0]) + sh_ref[0]
    hb = h.astype(BF16)
    k_ref[0] = _dot(hb, wk_ref[...]).astype(BF16)
    u_ref[0] = _dot(hb, wu_ref[...])
    ga_ref[0] = _dot(hb, wga_ref[...]).astype(BF16)
    gp_ref[0] = _dot(hb, wgp_ref[...]).astype(BF16)
    qT = _dot_nt(wqT_ref[...], hb).astype(BF16)
    for j in range(TOKEN_TILE // Q_TILE):
        qT_ref[0, j] = qT[:, j * Q_TILE:(j + 1) * Q_TILE]
    vT_ref[0] = _dot_nt(wvT_ref[...], hb).astype(BF16)


def _in_proj(x, sh1, sc1, wqT, wk, wvT, wu, wga, wgp):
    B, S, D = x.shape
    tm = TOKEN_TILE
    nq = tm // Q_TILE
    du = wu.shape[1]
    tok = lambda w: pl.BlockSpec((1, tm, w), lambda b, i: (b, i, 0))
    vec = pl.BlockSpec((1, 1, D), lambda b, i: (b, 0, 0))
    return pl.pallas_call(
        _in_proj_kernel,
        grid=(B, S // tm),
        in_specs=[tok(D), vec, vec,
                  _resident(wqT.shape), _resident(wk.shape), _resident(wvT.shape),
                  _resident(wu.shape), _resident(wga.shape), _resident(wgp.shape)],
        out_specs=[pl.BlockSpec((1, nq, D, Q_TILE), lambda b, i: (b, i, 0, 0)),
                   tok(D),
                   pl.BlockSpec((1, D, tm), lambda b, i: (b, 0, i)),
                   tok(du), tok(D), tok(D)],
        out_shape=[jax.ShapeDtypeStruct((B, S // Q_TILE, D, Q_TILE), BF16),
                   jax.ShapeDtypeStruct((B, S, D), BF16),
                   jax.ShapeDtypeStruct((B, D, S), BF16),
                   jax.ShapeDtypeStruct((B, S, du), F32),
                   jax.ShapeDtypeStruct((B, S, D), BF16),
                   jax.ShapeDtypeStruct((B, S, D), BF16)],
        compiler_params=_params(2),
        name="in_proj",
    )(x, sh1, sc1, wqT, wk, wvT, wu, wga, wgp)


def _bf16_parts(x):
    x1 = x.astype(BF16).astype(F32)
    x2 = (x - x1).astype(BF16).astype(F32)
    return x1, x2, x - x1 - x2


def _attn_kernel(hc_ref, lq1_ref, lk1_ref, lq2_ref, lk2_ref, subg_ref, qT_ref, k_ref, vT_ref,
                 o_ref, kpos_ref, diag_ref, vext_ref, *slots, seq, lambda_init):
    b = pl.program_id(0)
    h = pl.program_id(1)
    n_q = seq // Q_TILE
    assert KEY_CHUNK == Q_TILE
    z_refs, p_refs, m_refs = (slots[i * N_SLOTS:(i + 1) * N_SLOTS] for i in range(3))

    @pl.when((b == 0) & (h == 0))
    def _():
        j = lax.broadcasted_iota(jnp.int32, kpos_ref.shape, 0)
        l = lax.broadcasted_iota(jnp.int32, kpos_ref.shape, 1)
        kpos = jnp.where(l < 3, 1, jnp.where(l < 6, j - j % Q_TILE,
                                             jnp.where(l < 9, j % Q_TILE, 0)))
        kpos_ref[...] = kpos.astype(F32).astype(BF16)
        r = lax.broadcasted_iota(jnp.int32, diag_ref.shape, 0)
        c = lax.broadcasted_iota(jnp.int32, diag_ref.shape, 1)
        diag_ref[...] = jnp.abs(r - c).astype(F32)

    vext_ref[0:V_DIM, :] = vT_ref[0]
    vext_ref[V_DIM:, :] = jnp.ones((vext_ref.shape[0] - V_DIM, seq), BF16)

    lam = (jnp.exp(jnp.sum(lq1_ref[...] * lk1_ref[...], keepdims=True))
           - jnp.exp(jnp.sum(lq2_ref[...] * lk2_ref[...], keepdims=True))
           + lambda_init)
    out_gain = subg_ref[...] * (1.0 - lambda_init)

    c_bias = hc_ref[0, h]
    row = lax.broadcasted_iota(jnp.int32, (2 * HEAD_DIM, Q_TILE), 0)
    pos_row = lax.broadcasted_iota(jnp.int32, (POS_ROWS, Q_TILE), 0)
    pos_col = lax.broadcasted_iota(jnp.int32, (POS_ROWS, Q_TILE), 1)
    c_parts = _bf16_parts(jnp.full((POS_ROWS, Q_TILE), c_bias, F32))
    pos_pad = jnp.zeros((2 * HEAD_DIM - POS_ROWS, Q_TILE), BF16)
    n_chunks = seq // KEY_CHUNK
    n_items = 2 * n_q

    def scores_probs(si, pi):
        if si is not None:
            qi, mp_idx = divmod(si, 2)
            qs = qT_ref[0, qi].astype(F32) * (LOG2E * HEAD_DIM ** -0.5)
            keep = (row < HEAD_DIM) if mp_idx == 0 else (row >= HEAD_DIM)
            qm = jnp.where(keep, qs, 0.0).astype(BF16)
            ci_parts = _bf16_parts(-c_bias * (qi * Q_TILE + pos_col).astype(F32))
            qpos = jnp.zeros((POS_ROWS, Q_TILE), F32)
            for r, part in enumerate(ci_parts + c_parts + c_parts):
                qpos = jnp.where(pos_row == r, part, qpos)
            q_before = jnp.concatenate([qm, qpos.astype(BF16), pos_pad], axis=0)
            q_after = jnp.concatenate([qm, (-qpos).astype(BF16), pos_pad], axis=0)
            z_ref = z_refs[si % N_SLOTS]
        if pi is not None:
            zp_ref, p_ref = z_refs[pi % N_SLOTS], p_refs[pi % N_SLOTS]
            mp = m_refs[pi % N_SLOTS][...]
        m = None
        for c in range(n_chunks):
            rows = slice(c * KEY_CHUNK, (c + 1) * KEY_CHUNK)
            if si is not None:
                if c == qi:
                    z = _dot(k_ref[0, rows, :], qm) - c_bias * diag_ref[...]
                else:
                    kx = jnp.concatenate([k_ref[0, rows, :], kpos_ref[rows, :]], axis=1)
                    z = _dot(kx, q_before if c < qi else q_after)
                z_ref[rows, :] = z
                mc = jnp.max(z, axis=0, keepdims=True)
                m = mc if m is None else jnp.maximum(m, mc)
            if pi is not None:
                p_ref[rows, :] = jnp.exp2(zp_ref[rows, :] - mp).astype(BF16)
        if si is not None:
            m_refs[si % N_SLOTS][...] = m

    def values(vi, o_first):
        qi, mp_idx = divmod(vi, 2)
        acc = _dot(vext_ref[...], p_refs[vi % N_SLOTS][...])
        o = acc[0:V_DIM] / acc[V_DIM:V_DIM + 1]
        if mp_idx == 0:
            return o
        a = o_first - lam * o
        a = a * lax.rsqrt(jnp.mean(a * a, axis=0, keepdims=True) + RMS_EPS)
        o_ref[0, qi * Q_TILE:(qi + 1) * Q_TILE, :] = (a.T * out_gain).astype(BF16)
        return None

    o_first = None
    for t in range(n_items + 1):
        scores_probs(t if t < n_items else None, t - 1 if t >= 1 else None)
        if t >= 1:
            o_first = values(t - 1, o_first)


def _attn(qT, k, vT, lq1, lk1, lq2, lk2, sub_g, lambda_init):
    B, S, D = k.shape
    n_q = S // Q_TILE
    lam_spec = pl.BlockSpec((1, HEAD_DIM), lambda b, h: (0, 0))
    kern = functools.partial(_attn_kernel, seq=S, lambda_init=lambda_init)
    slopes = 2.0 ** -(np.arange(N_HEADS, dtype=np.float64) + 1.0)
    head_consts = jnp.asarray((slopes * LOG2E)[None, :], F32)
    return pl.pallas_call(
        kern,
        grid=(B, N_HEADS),
        in_specs=[pl.BlockSpec(memory_space=pltpu.SMEM),
                  lam_spec, lam_spec, lam_spec, lam_spec,
                  pl.BlockSpec((1, V_DIM), lambda b, h: (0, 0)),
                  pl.BlockSpec((1, n_q, V_DIM, Q_TILE), lambda b, h: (b, 0, h, 0)),
                  pl.BlockSpec((1, S, V_DIM), lambda b, h: (b, 0, h)),
                  pl.BlockSpec((1, V_DIM, S), lambda b, h: (b, h, 0))],
        out_specs=pl.BlockSpec((1, S, V_DIM), lambda b, h: (b, 0, h)),
        out_shape=jax.ShapeDtypeStruct((B, S, D), BF16),
        scratch_shapes=[pltpu.VMEM((S, 2 * HEAD_DIM), BF16),
                        pltpu.VMEM((Q_TILE, Q_TILE), F32),
                        pltpu.VMEM((V_DIM + 16, S), BF16),
                        *[pltpu.VMEM((S, Q_TILE), F32)] * N_SLOTS,
                        *[pltpu.VMEM((S, Q_TILE), BF16)] * N_SLOTS,
                        *[pltpu.VMEM((1, Q_TILE), F32)] * N_SLOTS],
        compiler_params=_params(2),
        name="attn",
    )(head_consts, lq1, lk1, lq2, lk2, sub_g, qT, k, vT)


def _pooled_groups(u_ref, top_ref, bot_ref, i, seq):
    tm = u_ref.shape[1]
    n = tm + 2 * POOL_PAD
    t = i * tm + lax.broadcasted_iota(jnp.int32, (tm, 1), 0)
    top = jnp.where(i > 0, top_ref[0], 0.0)
    bot = jnp.where(i < seq // tm - 1, bot_ref[0], 0.0)
    out = []
    for g, w in enumerate(POOL_WINDOWS):
        half = w // 2
        cols = slice(g * POOL_GROUP_IN, (g + 1) * POOL_GROUP_IN)
        ug = u_ref[0, :, cols]
        x = jnp.concatenate([top[:, cols], ug, bot[:, cols]], axis=0)
        win = x + pltpu.roll(x, 1, 0)
        step = 1
        while step < half:
            win = pltpu.roll(win, step, 0) + pltpu.roll(win, n - step, 0)
            step *= 2
        win = win[POOL_PAD:POOL_PAD + tm]
        cnt = (jnp.minimum(t + half, seq) - jnp.maximum(t - half, 0)).astype(F32)
        out.append((win / cnt - ug).astype(BF16))
    return out


def _mix_ffn_kernel(a_ref, u_ref, utop_ref, ubot_ref, ga_ref, gp_ref, x_ref,
                    g1_ref, sh2_ref, sc2_ref, g2_ref, wpool_ref, pscale_ref, wout_ref,
                    ln1g_ref, ln1b_ref, wg_ref, wu_ref, wo_ref, ln2g_ref, ln2b_ref,
                    o_ref, x1_carry, h_carry, *, seq, n_tiles):
    s = pl.program_id(0)
    tm = x_ref.shape[1]
    halves = (slice(0, tm // 2), slice(tm // 2, tm))
    slot = s % 2

    def pooled_groups():
        i = jnp.minimum(s, n_tiles - 1) % (seq // tm)
        return _pooled_groups(u_ref, utop_ref, ubot_ref, i, seq)

    def gate_mix(pooled, rows):
        p_out = jnp.concatenate([_dot(pooled[g][rows], wpool_ref[g])
                                 for g in range(len(POOL_WINDOWS))], axis=-1) * pscale_ref[...]
        mixed = (jax.nn.sigmoid(ga_ref[0, rows, :].astype(F32)) * a_ref[0, rows, :].astype(F32)
                 + jax.nn.sigmoid(gp_ref[0, rows, :].astype(F32)) * p_out)
        return mixed.astype(BF16)

    def out_proj_norm(rows, mixed):
        y = _dot(mixed, wout_ref[...])
        x1 = (_normalize(ALPHA * x_ref[0, rows, :] + g1_ref[0] * y) * ln1g_ref[...]
              + ln1b_ref[...])
        x1_carry[slot, rows, :] = x1
        h_carry[slot, rows, :] = (_normalize(x1) * (1.0 + sc2_ref[0]) + sh2_ref[0]).astype(BF16)

    def swiglu_chunk(h, y, chunk):
        start, width = chunk
        gt = _dot(h, wg_ref[:, start:start + width])
        up = _dot(h, wu_ref[:, start:start + width])
        act = (gt * jax.nn.sigmoid(gt) * up).astype(BF16)
        part = _dot(act, wo_ref[start:start + width, :])
        return part if y is None else y + part

    def finish(rows, y):
        x1 = x1_carry[1 - slot, rows, :]
        o_ref[0, rows, :] = (_normalize(ALPHA * x1 + g2_ref[0] * y) * ln2g_ref[...]
                             + ln2b_ref[...])

    @pl.when(s == 0)
    def _():
        pooled = pooled_groups()
        for rows in halves:
            out_proj_norm(rows, gate_mix(pooled, rows))

    @pl.when((s > 0) & (s < n_tiles))
    def _():
        pooled = pooled_groups()
        for rows in halves:
            h = h_carry[1 - slot, rows, :]
            y = swiglu_chunk(h, None, FF_CHUNKS[0])
            mixed = gate_mix(pooled, rows)
            y = swiglu_chunk(h, y, FF_CHUNKS[1])
            out_proj_norm(rows, mixed)
            for chunk in FF_CHUNKS[2:]:
                y = swiglu_chunk(h, y, chunk)
            finish(rows, y)

    @pl.when(s == n_tiles)
    def _():
        for rows in halves:
            h = h_carry[1 - slot, rows, :]
            y = None
            for chunk in FF_CHUNKS:
                y = swiglu_chunk(h, y, chunk)
            finish(rows, y)


def _mix_ffn(a, u, ga, gp, x, g1, sh2, sc2, g2, *weights):
    B, S, D = x.shape
    tm = TOKEN_TILE
    nt = S // tm
    n_tiles = B * nt
    pad_blocks = tm // POOL_PAD
    du = u.shape[-1]

    def front(s):
        t = jnp.minimum(s, n_tiles - 1)
        return t // nt, t % nt

    def back(s):
        t = jnp.maximum(s - 1, 0)
        return t // nt, t % nt

    tok = lambda w: pl.BlockSpec((1, tm, w), lambda s: (*front(s), 0))
    vec = pl.BlockSpec((1, 1, D), lambda s: (front(s)[0], 0, 0))
    top = pl.BlockSpec((1, POOL_PAD, du), lambda s: (
        front(s)[0], jnp.maximum(front(s)[1] * pad_blocks - 1, 0), 0))
    bot = pl.BlockSpec((1, POOL_PAD, du), lambda s: (
        front(s)[0], jnp.minimum((front(s)[1] + 1) * pad_blocks, S // POOL_PAD - 1), 0))
    return pl.pallas_call(
        functools.partial(_mix_ffn_kernel, seq=S, n_tiles=n_tiles),
        grid=(n_tiles + 1,),
        in_specs=[tok(D), tok(du), top, bot, tok(D), tok(D), tok(D), vec, vec, vec,
                  pl.BlockSpec((1, 1, D), lambda s: (back(s)[0], 0, 0)),
                  *[_resident(w.shape) for w in weights]],
        out_specs=pl.BlockSpec((1, tm, D), lambda s: (*back(s), 0)),
        out_shape=jax.ShapeDtypeStruct((B, S, D), F32),
        scratch_shapes=[pltpu.VMEM((2, tm, D), F32), pltpu.VMEM((2, tm, D), BF16)],
        compiler_params=_params(1),
        name="mix_ffn",
    )(a, u, u, u, ga, gp, x, g1, sh2, sc2, g2, *weights)


def kernel(x, c, w_ada, b_ada, w_in, lambda_q1, lambda_k1, lambda_q2, lambda_k2, sub_g, w_pool,
           pool_scale, w_out, ln1_g, ln1_b, w_ffn_in, w_ffn_out, ln2_g, ln2_b):
    B, S, D = x.shape
    assert w_ada.shape[0] == DEPTH == 1
    assert S % TOKEN_TILE == 0 and TOKEN_TILE % Q_TILE == 0
    d_ff = w_ffn_out.shape[1]
    assert sum(w for _, w in FF_CHUNKS) == d_ff
    pool_in = len(POOL_WINDOWS) * POOL_GROUP_IN
    qk = N_HEADS * 2 * HEAD_DIM
    lambda_init = 0.8 - 0.6 * math.exp(-0.3 * 0)

    mod = _mod(c, w_ada[0], b_ada[0])
    sh1, sc1, g1, sh2, sc2, g2 = [m.reshape(B, 1, D) for m in jnp.split(mod, N_MOD, axis=-1)]

    w = w_in[0].astype(BF16)
    bounds = np.cumsum([qk, qk, D, pool_in, D]).tolist()
    wq, wk, wv, wu, wga, wgp = jnp.split(w, bounds, axis=-1)
    qT, k, vT, u, ga, gp = _in_proj(x, sh1, sc1, wq.T, wk, wv.T, wu, wga, wgp)

    row = lambda v: v.reshape(1, -1)
    a = _attn(qT, k, vT, row(lambda_q1[0]), row(lambda_k1[0]), row(lambda_q2[0]),
              row(lambda_k2[0]), row(sub_g[0]), lambda_init)
    wf = w_ffn_in[0].astype(BF16)
    return _mix_ffn(a, u, ga, gp, x, g1, sh2, sc2, g2,
                    w_pool[0].astype(BF16), row(pool_scale[0]), w_out[0].astype(BF16),
                    row(ln1_g[0]), row(ln1_b[0]),
                    wf[:, :d_ff], wf[:, d_ff:], w_ffn_out[0].astype(BF16),
                    row(ln2_g[0]), row(ln2_b[0]))
```
